```python
import math
import jax, jax.numpy as jnp
from jax import lax
import numpy as np

D_MODEL = 1024
BATCH = 4
SEQ = 4096
DEPTH = 4

N_META = 16
BLOCK = 128
N_A_LAYERS = DEPTH // 2
N_B_LAYERS = DEPTH - N_A_LAYERS
CONV_WIDTH = 31
N_HEADS = 8
HEAD_DIM = D_MODEL // (2 * N_HEADS)
V_DIM = 2 * HEAD_DIM
ROT_DIM = HEAD_DIM // 4
ROPE_THETA = 500000.0
PEER_HEADS = 8
PEER_KEYS = 128
PEER_EXPERTS = PEER_KEYS * PEER_KEYS
PEER_HALF = 128
PEER_QDIM = 2 * PEER_HALF
PEER_TOPK = 16
PEER_CHUNK = 128
EPS = 1e-6

kernel_name = "yoco_conformer_diffattn_peer"


def rmsnorm(x, g):
    xf = x.astype(jnp.float32)
    y = xf * lax.rsqrt(jnp.mean(xf * xf, axis=-1, keepdims=True) + EPS)
    return (y * g.astype(jnp.float32)).astype(x.dtype)


def layernorm(x, g, b):
    xf = x.astype(jnp.float32)
    mu = jnp.mean(xf, axis=-1, keepdims=True)
    xc = xf - mu
    y = xc * lax.rsqrt(jnp.mean(xc * xc, axis=-1, keepdims=True) + EPS)
    return (y * g.astype(jnp.float32) + b.astype(jnp.float32)).astype(x.dtype)


def partial_rope(t, pos):
    inv = ROPE_THETA ** (-jnp.arange(0, ROT_DIM, 2, dtype=jnp.float32) / ROT_DIM)
    ang = pos.astype(jnp.float32)[:, None] * inv[None, :]
    cos = jnp.cos(ang)[None, :, None, None, :]
    sin = jnp.sin(ang)[None, :, None, None, :]
    tr = t[..., :ROT_DIM].astype(jnp.float32)
    t1, t2 = tr[..., :ROT_DIM // 2], tr[..., ROT_DIM // 2:]
    rot = jnp.concatenate([t1 * cos - t2 * sin, t1 * sin + t2 * cos], axis=-1).astype(t.dtype)
    return jnp.concatenate([rot, t[..., ROT_DIM:]], axis=-1)


def conformer_conv(h, w1, b1, dw_w, dw_b, ln_g, ln_b, w2, b2):
    u = h @ w1 + b1
    a, g = jnp.split(u, 2, axis=-1)
    u = a * jax.nn.sigmoid(g)
    u = lax.conv_general_dilated(
        u, dw_w[:, None, :], window_strides=(1,), padding=[(CONV_WIDTH - 1, 0)],
        dimension_numbers=('NWC', 'WIO', 'NWC'), feature_group_count=D_MODEL) + dw_b
    u = jax.nn.silu(layernorm(u, ln_g, ln_b))
    return u @ w2 + b2


def peer(h, wq, sk1, sk2, u_tab, v_tab):
    B, L, D = h.shape
    tok = h.reshape(-1, PEER_CHUNK, D)

    def chunk(xc):
        q = (xc @ wq).reshape(PEER_CHUNK, PEER_HEADS, 2, PEER_HALF)
        s1 = jnp.einsum('chd,kd->chk', q[:, :, 0], sk1)
        s2 = jnp.einsum('chd,kd->chk', q[:, :, 1], sk2)
        v1, i1 = lax.top_k(s1, PEER_TOPK)
        v2, i2 = lax.top_k(s2, PEER_TOPK)
        cand_s = (v1[..., :, None] + v2[..., None, :]).reshape(PEER_CHUNK, PEER_HEADS, PEER_TOPK * PEER_TOPK)
        cand_i = (i1[..., :, None] * PEER_KEYS + i2[..., None, :]).reshape(PEER_CHUNK, PEER_HEADS, PEER_TOPK * PEER_TOPK)
        top_s, top_pos = lax.top_k(cand_s, PEER_TOPK)
        expert = jnp.take_along_axis(cand_i, top_pos, axis=-1)
        gate = jax.nn.softmax(top_s.astype(jnp.float32), axis=-1).astype(xc.dtype)
        u = u_tab[expert]
        v = v_tab[expert]
        act = jax.nn.gelu(jnp.einsum('cd,chkd->chk', xc, u), approximate=False)
        return jnp.einsum('chk,chkd->cd', gate * act, v)

    return lax.map(chunk, tok).reshape(B, L, D)


def diff_attention(h, wq, k, v, lq1, lk1, lq2, lk2, subln_g, wo, lam_init, pos):
    B, L, D = h.shape
    q = (h @ wq).reshape(B, L, N_HEADS, 2, HEAD_DIM)
    q = partial_rope(q, pos) * (HEAD_DIM ** -0.5)
    lam = (jnp.exp(jnp.sum(lq1.astype(jnp.float32) * lk1.astype(jnp.float32)))
           - jnp.exp(jnp.sum(lq2.astype(jnp.float32) * lk2.astype(jnp.float32))) + lam_init)
    n_blk = L // BLOCK
    qb = q.reshape(B, n_blk, BLOCK, N_HEADS, 2, HEAD_DIM).transpose(1, 0, 2, 3, 4, 5)
    kpos = jnp.arange(L)

    def attend(args):
        qblk, bi = args
        s = jnp.einsum('bqhcd,bkhcd->bhcqk', qblk, k).astype(jnp.float32)
        qpos = bi * BLOCK + jnp.arange(BLOCK)
        mask = kpos[None, :] <= qpos[:, None]
        s = jnp.where(mask, s, -jnp.inf)
        p = jax.nn.softmax(s, axis=-1)
        a = p[:, :, 0] - lam * p[:, :, 1]
        return jnp.einsum('bhqk,bkhe->bqhe', a.astype(v.dtype), v)

    o = lax.map(attend, (qb, jnp.arange(n_blk)))
    o = o.transpose(1, 0, 2, 3, 4).reshape(B, L, N_HEADS, V_DIM)
    o = rmsnorm(o, subln_g) * (1.0 - lam_init)
    return o.reshape(B, L, N_HEADS * V_DIM) @ wo


def setup_inputs(seed: int = 0) -> dict:
    key = jax.random.key(seed)
    ks = jax.random.split(key, 32)
    D = D_MODEL
    nrm = lambda k, shape, scale: jax.random.normal(k, shape, jnp.float32) * scale
    gain = lambda k, shape: 1.0 + 0.01 * jax.random.normal(k, shape, jnp.float32)
    return {
        "x": nrm(ks[0], (BATCH, SEQ, D), 1.0),
        "meta_tokens": nrm(ks[1], (N_META, D), 1.0),
        "a_norm_g": gain(ks[2], (N_A_LAYERS, D)),
        "a_pw1_w": nrm(ks[3], (N_A_LAYERS, D, 2 * D), D ** -0.5),
        "a_pw1_b": nrm(ks[4], (N_A_LAYERS, 2 * D), 0.01),
        "a_dw_w": nrm(ks[5], (N_A_LAYERS, CONV_WIDTH, D), CONV_WIDTH ** -0.5),
        "a_dw_b": nrm(ks[6], (N_A_LAYERS, D), 0.01),
        "a_ln_g": gain(ks[7], (N_A_LAYERS, D)),
        "a_ln_b": nrm(ks[8], (N_A_LAYERS, D), 0.01),
        "a_pw2_w": nrm(ks[9], (N_A_LAYERS, D, D), D ** -0.5),
        "a_pw2_b": nrm(ks[10], (N_A_LAYERS, D), 0.01),
        "kv_norm_g": gain(ks[11], (D,)),
        "w_kv": nrm(ks[12], (D, 2 * N_HEADS * HEAD_DIM + N_HEADS * V_DIM), D ** -0.5),
        "b_norm_g": gain(ks[13], (N_B_LAYERS, D)),
        "b_wq": nrm(ks[14], (N_B_LAYERS, D, 2 * N_HEADS * HEAD_DIM), D ** -0.5),
        "b_lambda_q1": nrm(ks[15], (N_B_LAYERS, HEAD_DIM), 0.1),
        "b_lambda_k1": nrm(ks[16], (N_B_LAYERS, HEAD_DIM), 0.1),
        "b_lambda_q2": nrm(ks[17], (N_B_LAYERS, HEAD_DIM), 0.1),
        "b_lambda_k2": nrm(ks[18], (N_B_LAYERS, HEAD_DIM), 0.1),
        "b_subln_g": gain(ks[19], (N_B_LAYERS, V_DIM)),
        "b_wo": nrm(ks[20], (N_B_LAYERS, N_HEADS * V_DIM, D), D ** -0.5),
        "f_norm_g": gain(ks[21], (DEPTH, D)),
        "f_wq": nrm(ks[22], (DEPTH, D, PEER_HEADS * PEER_QDIM), D ** -0.5),
        "f_subkey1": nrm(ks[23], (DEPTH, PEER_KEYS, PEER_HALF), PEER_HALF ** -0.5),
        "f_subkey2": nrm(ks[24], (DEPTH, PEER_KEYS, PEER_HALF), PEER_HALF ** -0.5),
        "f_u": nrm(ks[25], (DEPTH, PEER_EXPERTS, D), D ** -0.5),
        "f_v": nrm(ks[26], (DEPTH, PEER_EXPERTS, D), D ** -0.5),
        "final_norm_g": gain(ks[27], (D,)),
    }


def reference(x, meta_tokens, a_norm_g, a_pw1_w, a_pw1_b, a_dw_w, a_dw_b, a_ln_g, a_ln_b,
              a_pw2_w, a_pw2_b, kv_norm_g, w_kv, b_norm_g, b_wq, b_lambda_q1, b_lambda_k1,
              b_lambda_q2, b_lambda_k2, b_subln_g, b_wo, f_norm_g, f_wq, f_subkey1, f_subkey2,
              f_u, f_v, final_norm_g):
    B, S, D = x.shape
    L = N_META + S
    Lp = -(-L // BLOCK) * BLOCK
    meta = jnp.broadcast_to(meta_tokens.astype(x.dtype)[None], (B, N_META, D))
    pad = jnp.zeros((B, Lp - L, D), x.dtype)
    h = jnp.concatenate([meta, x, pad], axis=1)
    pos = jnp.arange(Lp)
    k_shared = None
    v_shared = None
    for layer in range(DEPTH):
        if layer < N_A_LAYERS:
            i = layer
            h = h + conformer_conv(rmsnorm(h, a_norm_g[i]), a_pw1_w[i], a_pw1_b[i], a_dw_w[i],
                                   a_dw_b[i], a_ln_g[i], a_ln_b[i], a_pw2_w[i], a_pw2_b[i])
        else:
            j = layer - N_A_LAYERS
            lam_init = 0.8 - 0.6 * math.exp(-0.3 * layer)
            h = h + diff_attention(rmsnorm(h, b_norm_g[j]), b_wq[j], k_shared, v_shared,
                                   b_lambda_q1[j], b_lambda_k1[j], b_lambda_q2[j], b_lambda_k2[j],
                                   b_subln_g[j], b_wo[j], lam_init, pos)
        h = h + peer(rmsnorm(h, f_norm_g[layer]), f_wq[layer], f_subkey1[layer],
                     f_subkey2[layer], f_u[layer], f_v[layer])
        if layer == N_A_LAYERS - 1:
            kv = rmsnorm(h, kv_norm_g) @ w_kv
            k_flat, v_flat = jnp.split(kv, [2 * N_HEADS * HEAD_DIM], axis=-1)
            k_shared = partial_rope(k_flat.reshape(B, Lp, N_HEADS, 2, HEAD_DIM), pos)
            v_shared = v_flat.reshape(B, Lp, N_HEADS, V_DIM)
    out = rmsnorm(h, final_norm_g)
    return out[:, N_META:N_META + S]
```

```python
import functools
import math

import jax
import jax.numpy as jnp
from jax import lax
from jax.experimental import pallas as pl
from jax.experimental.pallas import tpu as pltpu

D_MODEL = 1024
N_META = 16
BLOCK = 128
DEPTH = 4
N_A_LAYERS = DEPTH // 2
CONV_WIDTH = 31
N_HEADS = 8
HEAD_DIM = 64
V_DIM = 128
ROT_DIM = 16
ROPE_THETA = 500000.0
PEER_HEADS = 8
PEER_KEYS = 128
PEER_EXPERTS = PEER_KEYS * PEER_KEYS
PEER_TOPK = 16
EPS = 1e-6

NEG = -1e30
VMEM_LIMIT = 52 * 1024 * 1024

F32 = jnp.float32
BF16 = jnp.bfloat16


def _cparams(sem):
    return pltpu.CompilerParams(dimension_semantics=sem, vmem_limit_bytes=VMEM_LIMIT)


def _rms(x, g):
    return x * lax.rsqrt(jnp.mean(x * x, axis=-1, keepdims=True) + EPS) * g


def _dot(a, b):
    return jnp.dot(a, b, preferred_element_type=F32)


def _dot_nt(a, b):
    return lax.dot_general(a, b, (((1,), (1,)), ((), ())), preferred_element_type=F32)


CONF_TS = 384
CONF_HALO = 32
CONF_RB = 32
CONF_CB = 256


def _conformer_kernel(x_ref, ng_ref, w1a_ref, w1g_ref, b1a_ref, b1g_ref, dww_ref, dwb_ref,
                      lng_ref, lnb_ref, w2_ref, b2_ref, o_ref, buf_ref, cbuf_ref):
    ts = CONF_TS

    @pl.when(pl.program_id(1) == 0)
    def _():
        buf_ref[0:CONF_HALO, :] = jnp.zeros((CONF_HALO, D_MODEL), F32)

    x = x_ref[0]
    n = _rms(x, ng_ref[...]).astype(BF16)
    a = _dot(n, w1a_ref[...]) + b1a_ref[...]
    g = _dot(n, w1g_ref[...]) + b1g_ref[...]
    buf_ref[CONF_HALO:CONF_HALO + ts, :] = a * jax.nn.sigmoid(g)

    off0 = CONF_HALO - (CONV_WIDTH - 1)

    def row_body(rb, carry):
        r0 = pl.multiple_of(rb * CONF_RB, CONF_RB)
        for cb in range(D_MODEL // CONF_CB):
            cols = slice(cb * CONF_CB, (cb + 1) * CONF_CB)
            acc = jnp.broadcast_to(dwb_ref[:, cols], (CONF_RB, CONF_CB))
            win = buf_ref[pl.ds(r0, CONF_RB + CONF_HALO), cols]
            for k in range(CONV_WIDTH):
                acc = acc + dww_ref[k:k + 1, cols] * win[off0 + k:off0 + k + CONF_RB]
            cbuf_ref[pl.ds(r0, CONF_RB), cols] = acc
        return carry

    lax.fori_loop(0, ts // CONF_RB, row_body, 0)
    buf_ref[0:CONF_HALO, :] = buf_ref[ts:ts + CONF_HALO, :]

    c = cbuf_ref[...]
    mu = jnp.mean(c, axis=-1, keepdims=True)
    cc = c - mu
    y = cc * lax.rsqrt(jnp.mean(cc * cc, axis=-1, keepdims=True) + EPS)
    y = y * lng_ref[...] + lnb_ref[...]
    y = (y * jax.nn.sigmoid(y)).astype(BF16)
    o_ref[0] = x + _dot(y, w2_ref[...]) + b2_ref[...]


def _conformer_layer(h3, ng, w1, b1, dww, dwb, lng, lnb, w2, b2):
    B, Lp, D = h3.shape
    ts = CONF_TS
    row = lambda v: v.reshape(1, -1).astype(F32)
    const = lambda shape: pl.BlockSpec(shape, lambda b, t: (0,) * len(shape))
    return pl.pallas_call(
        _conformer_kernel,
        grid=(B, Lp // ts),
        in_specs=[
            pl.BlockSpec((1, ts, D), lambda b, t: (b, t, 0)),
            const((1, D)), const((D, D)), const((D, D)), const((1, D)), const((1, D)),
            const((CONV_WIDTH, D)), const((1, D)), const((1, D)), const((1, D)),
            const((D, D)), const((1, D)),
        ],
        out_specs=pl.BlockSpec((1, ts, D), lambda b, t: (b, t, 0)),
        out_shape=jax.ShapeDtypeStruct((B, Lp, D), F32),
        scratch_shapes=[pltpu.VMEM((ts + CONF_HALO, D), F32), pltpu.VMEM((ts, D), F32)],
        compiler_params=_cparams(("arbitrary", "arbitrary")),
        name="conformer",
    )(h3, row(ng), w1[:, :D].astype(BF16), w1[:, D:].astype(BF16), row(b1[:D]), row(b1[D:]),
      dww.astype(F32), row(dwb), row(lng), row(lnb), w2.astype(BF16), row(b2))


PROJ_TS = 384


def _proj_kernel(x_ref, g_ref, w_ref, cos_ref, sin_ref, o_ref, *, rope, scale):
    n = _rms(x_ref[...], g_ref[...]).astype(BF16)
    y = _dot(n, w_ref[...])
    if rope:
        cos = cos_ref[...]
        sin = sin_ref[...]
        lane = lax.broadcasted_iota(jnp.int32, cos.shape, 1) % HEAD_DIM
        for cb in range(y.shape[1] // 128):
            yb = y[:, cb * 128:(cb + 1) * 128]
            fwd = pltpu.roll(yb, 128 - ROT_DIM // 2, axis=1)
            bwd = pltpu.roll(yb, ROT_DIM // 2, axis=1)
            partner = jnp.where(lane < ROT_DIM // 2, fwd, bwd)
            o_ref[:, cb * 128:(cb + 1) * 128] = ((yb * cos + partner * sin) * scale).astype(BF16)
    else:
        o_ref[...] = (y * scale).astype(BF16)


def _norm_proj(h2, g, w, cos_t, sin_t, *, rope, scale, lp):
    T, D = h2.shape
    N = w.shape[1]
    ts = PROJ_TS
    nt = lp // ts
    return pl.pallas_call(
        functools.partial(_proj_kernel, rope=rope, scale=scale),
        grid=(T // ts,),
        in_specs=[
            pl.BlockSpec((ts, D), lambda i: (i, 0)),
            pl.BlockSpec((1, D), lambda i: (0, 0)),
            pl.BlockSpec((D, N), lambda i: (0, 0)),
            pl.BlockSpec((ts, 128), lambda i: (i % nt, 0)),
            pl.BlockSpec((ts, 128), lambda i: (i % nt, 0)),
        ],
        out_specs=pl.BlockSpec((ts, N), lambda i: (i, 0)),
        out_shape=jax.ShapeDtypeStruct((T, N), BF16),
        compiler_params=_cparams(("arbitrary",)),
        name="norm_proj",
    )(h2, g.reshape(1, D).astype(F32), w.astype(BF16), cos_t, sin_t)


def _rope_tables(lp):
    inv = ROPE_THETA ** (-jnp.arange(0, ROT_DIM, 2, dtype=F32) / ROT_DIM)
    d = jnp.arange(128) % HEAD_DIM
    ang = jnp.arange(lp, dtype=F32)[:, None] * inv[d % (ROT_DIM // 2)][None, :]
    cos_t = jnp.where(d[None, :] < ROT_DIM, jnp.cos(ang), 1.0)
    sin_t = jnp.where(d[None, :] < ROT_DIM // 2, -jnp.sin(ang),
                      jnp.where(d[None, :] < ROT_DIM, jnp.sin(ang), 0.0))
    return cos_t.astype(F32), sin_t.astype(F32)


ATT_T = 384


def _attn_kernel(q_ref, k_ref, v_ref, lam_ref, sg_ref, o_ref, *, lam_init):
    tq = ATT_T
    qi = pl.program_id(2)
    q = q_ref[0]
    lane = lax.broadcasted_iota(jnp.int32, q.shape, 1)
    zero = jnp.zeros_like(q)
    qq = jnp.concatenate([jnp.where(lane < HEAD_DIM, q, zero),
                          jnp.where(lane >= HEAD_DIM, q, zero)], axis=0)

    def step(kc, carry, masked):
        m, l, acc = carry
        k0 = pl.multiple_of(kc * tq, tq)
        k = k_ref[0, pl.ds(k0, tq), :]
        v = v_ref[0, pl.ds(k0, tq), :]
        s = _dot_nt(qq, k)
        if masked:
            r = lax.broadcasted_iota(jnp.int32, s.shape, 0)
            r = jnp.where(r >= tq, r - tq, r)
            c = lax.broadcasted_iota(jnp.int32, s.shape, 1)
            s = jnp.where(c <= r, s, NEG)
        m_new = jnp.maximum(m, jnp.max(s, axis=-1, keepdims=True))
        alpha = jnp.exp(m - m_new)
        p = jnp.exp(s - m_new)
        l = alpha * l + jnp.sum(p, axis=-1, keepdims=True)
        acc = alpha * acc + _dot(p.astype(BF16), v)
        return m_new, l, acc

    init = (jnp.full((2 * tq, 1), NEG, F32), jnp.zeros((2 * tq, 1), F32),
            jnp.zeros((2 * tq, V_DIM), F32))
    carry = lax.fori_loop(0, qi, lambda kc, c: step(kc, c, False), init)
    m, l, acc = step(qi, carry, True)
    o = acc / l
    lp = lam_ref[...]
    lam = (jnp.exp(jnp.sum(lp[0:1] * lp[1:2], axis=-1, keepdims=True))
           - jnp.exp(jnp.sum(lp[2:3] * lp[3:4], axis=-1, keepdims=True)) + lam_init)
    od = o[:tq] - lam * o[tq:]
    od = od * lax.rsqrt(jnp.mean(od * od, axis=-1, keepdims=True) + EPS) * sg_ref[...]
    o_ref[0] = (od * (1.0 - lam_init)).astype(BF16)


def _diff_attention(q3, k3, v3, lam_rows, subln_g, lam_init):
    B, Lp, _ = q3.shape
    tq = ATT_T
    return pl.pallas_call(
        functools.partial(_attn_kernel, lam_init=lam_init),
        grid=(B, N_HEADS, Lp // tq),
        in_specs=[
            pl.BlockSpec((1, tq, 128), lambda b, h, i: (b, i, h)),
            pl.BlockSpec((1, Lp, 128), lambda b, h, i: (b, 0, h)),
            pl.BlockSpec((1, Lp, 128), lambda b, h, i: (b, 0, h)),
            pl.BlockSpec((8, 128), lambda b, h, i: (0, 0)),
            pl.BlockSpec((1, V_DIM), lambda b, h, i: (0, 0)),
        ],
        out_specs=pl.BlockSpec((1, tq, 128), lambda b, h, i: (b, i, h)),
        out_shape=jax.ShapeDtypeStruct((B, Lp, N_HEADS * V_DIM), BF16),
        compiler_params=_cparams(("arbitrary", "arbitrary", "arbitrary")),
        name="diff_attn",
    )(q3, k3, v3, lam_rows, subln_g.reshape(1, V_DIM).astype(F32))


def _matmul_res_kernel(a_ref, w_ref, h_ref, o_ref):
    o_ref[...] = h_ref[...] + _dot(a_ref[...], w_ref[...])


def _matmul_res(a2, w, h2):
    T, K = a2.shape
    N = w.shape[1]
    ts = PROJ_TS
    return pl.pallas_call(
        _matmul_res_kernel,
        grid=(T // ts,),
        in_specs=[pl.BlockSpec((ts, K), lambda i: (i, 0)),
                  pl.BlockSpec((K, N), lambda i: (0, 0)),
                  pl.BlockSpec((ts, N), lambda i: (i, 0))],
        out_specs=pl.BlockSpec((ts, N), lambda i: (i, 0)),
        out_shape=jax.ShapeDtypeStruct((T, N), F32),
        compiler_params=_cparams(("arbitrary",)),
        name="out_proj",
    )(a2, w.astype(BF16), h2)


ROUTE_TS = 256


def _top_values(s, n):
    rows = lax.broadcasted_iota(jnp.int32, (n, s.shape[1]), 0)
    vals = jnp.full((n, s.shape[1]), NEG, F32)
    work = s
    for j in range(n):
        m = jnp.max(work, axis=0, keepdims=True)
        vals = jnp.where(rows == j, m, vals)
        if j + 1 < n:
            work = jnp.where(work >= m, NEG, work)
    return vals


def _candidate_sums(v1, v2):
    k = PEER_TOPK
    row8 = lax.broadcasted_iota(jnp.int32, (8, v1.shape[1]), 0)
    blocks = [v1[0:1] + v2]
    for a in range(1, 8):
        nb = (k + 1) // (a + 1)
        blocks.append(jnp.where(row8 < nb, v1[a:a + 1] + v2[0:8], NEG))
    blocks.append(v1[8:16] + v2[0:1])
    return jnp.concatenate(blocks, axis=0)


def _route_kernel(h_ref, g_ref, wq_ref, sk1_ref, sk2_ref, xn_ref, c_ref, s2_ref, th_ref, r_ref):
    xn = _rms(h_ref[...], g_ref[...]).astype(BF16)
    xn_ref[...] = xn
    q = _dot(xn, wq_ref[...]).astype(BF16)
    for hd in range(PEER_HEADS):
        q1 = q[:, hd * 256:hd * 256 + 128]
        q2 = q[:, hd * 256 + 128:(hd + 1) * 256]
        s1 = _dot_nt(sk1_ref[...], q1)
        s2 = _dot_nt(sk2_ref[...], q2)
        v1 = _top_values(s1, PEER_TOPK)
        v2 = _top_values(s2, PEER_TOPK)
        tops = _top_values(_candidate_sums(v1, v2), PEER_TOPK + 1)
        z = jnp.sum(jnp.exp(tops[0:PEER_TOPK] - tops[0:1]), axis=0, keepdims=True)
        tau = 0.5 * (tops[PEER_TOPK - 1:PEER_TOPK] + tops[PEER_TOPK:PEER_TOPK + 1])
        th_ref[hd] = tau - s1
        r_ref[hd] = jnp.exp(s1 - v1[0:1]) / z
        s2_ref[hd] = s2
        c_ref[hd] = jnp.exp(s2 - v2[0:1])


def _peer_route(h2, g, wq, sk1, sk2):
    T, D = h2.shape
    ts = ROUTE_TS
    nq = wq.shape[1]
    tab = jax.ShapeDtypeStruct((PEER_HEADS, PEER_KEYS, T), F32)
    tab_spec = pl.BlockSpec((PEER_HEADS, PEER_KEYS, ts), lambda i: (0, 0, i))
    return pl.pallas_call(
        _route_kernel,
        grid=(T // ts,),
        in_specs=[pl.BlockSpec((ts, D), lambda i: (i, 0)),
                  pl.BlockSpec((1, D), lambda i: (0, 0)),
                  pl.BlockSpec((D, nq), lambda i: (0, 0)),
                  pl.BlockSpec((PEER_KEYS, 128), lambda i: (0, 0)),
                  pl.BlockSpec((PEER_KEYS, 128), lambda i: (0, 0))],
        out_specs=[pl.BlockSpec((ts, D), lambda i: (i, 0)), tab_spec, tab_spec, tab_spec, tab_spec],
        out_shape=[jax.ShapeDtypeStruct((T, D), BF16), tab, tab, tab, tab],
        compiler_params=_cparams(("arbitrary",)),
        name="peer_route",
    )(h2, g.reshape(1, D).astype(F32), wq.astype(BF16), sk1.astype(BF16), sk2.astype(BF16))


PEER_TT = 512
PEER_ET = 1024
SQRT_HALF = math.sqrt(0.5)


def _peer_dense_kernel(xn_ref, u_ref, vt_ref, c_ref, s2_ref, th_ref, r_ref, h_ref, o_ref,
                       acc_ref, p_ref):
    j = pl.program_id(1)

    @pl.when(j == 0)
    def _():
        acc_ref[...] = jnp.zeros_like(acc_ref)

    z = _dot_nt(u_ref[...], xn_ref[...])
    for ii in range(PEER_ET // PEER_KEYS):
        i1 = j * (PEER_ET // PEER_KEYS) + ii
        w = jnp.zeros((PEER_KEYS, PEER_TT), F32)
        for hd in range(PEER_HEADS):
            th = th_ref[hd, pl.ds(i1, 1), :]
            rr = r_ref[hd, pl.ds(i1, 1), :]
            w = w + rr * jnp.where(s2_ref[hd] >= th, c_ref[hd], 0.0)
        zz = z[ii * PEER_KEYS:(ii + 1) * PEER_KEYS]
        act = 0.5 * zz * (1.0 + lax.erf(zz * SQRT_HALF))
        p_ref[ii * PEER_KEYS:(ii + 1) * PEER_KEYS, :] = (w * act).astype(BF16)
    acc_ref[...] += _dot(vt_ref[...], p_ref[...])

    @pl.when(j == pl.num_programs(1) - 1)
    def _():
        o_ref[...] = h_ref[...] + acc_ref[...].T


def _peer_dense(xn, u_bf, vt_bf, c, s2, th, r, h2):
    T, D = h2.shape
    tt, et = PEER_TT, PEER_ET
    tab_spec = pl.BlockSpec((PEER_HEADS, PEER_KEYS, tt), lambda i, j: (0, 0, i))
    return pl.pallas_call(
        _peer_dense_kernel,
        grid=(T // tt, PEER_EXPERTS // et),
        in_specs=[pl.BlockSpec((tt, D), lambda i, j: (i, 0)),
                  pl.BlockSpec((et, D), lambda i, j: (j, 0)),
                  pl.BlockSpec((D, et), lambda i, j: (0, j)),
                  tab_spec, tab_spec, tab_spec, tab_spec,
                  pl.BlockSpec((tt, D), lambda i, j: (i, 0))],
        out_specs=pl.BlockSpec((tt, D), lambda i, j: (i, 0)),
        out_shape=jax.ShapeDtypeStruct((T, D), F32),
        scratch_shapes=[pltpu.VMEM((D, tt), F32), pltpu.VMEM((et, tt), BF16)],
        compiler_params=_cparams(("arbitrary", "arbitrary")),
        name="peer_dense",
    )(xn, u_bf, vt_bf, c, s2, th, r, h2)


def _peer_layer(h2, g, wq, sk1, sk2, u_tab, v_tab):
    xn, c, s2, th, r = _peer_route(h2, g, wq, sk1, sk2)
    return _peer_dense(xn, u_tab.astype(BF16), v_tab.astype(BF16).T, c, s2, th, r, h2)


def _final_norm_kernel(x_ref, g_ref, o_ref):
    o_ref[...] = _rms(x_ref[...], g_ref[...])


def _final_norm(h2, g):
    T, D = h2.shape
    ts = PROJ_TS
    return pl.pallas_call(
        _final_norm_kernel,
        grid=(T // ts,),
        in_specs=[pl.BlockSpec((ts, D), lambda i: (i, 0)), pl.BlockSpec((1, D), lambda i: (0, 0))],
        out_specs=pl.BlockSpec((ts, D), lambda i: (i, 0)),
        out_shape=jax.ShapeDtypeStruct((T, D), F32),
        compiler_params=_cparams(("arbitrary",)),
        name="final_norm",
    )(h2, g.reshape(1, D).astype(F32))


def kernel(x, meta_tokens, a_norm_g, a_pw1_w, a_pw1_b, a_dw_w, a_dw_b, a_ln_g, a_ln_b, a_pw2_w, a_pw2_b, kv_norm_g, w_kv, b_norm_g, b_wq, b_lambda_q1, b_lambda_k1, b_lambda_q2, b_lambda_k2, b_subln_g, b_wo, f_norm_g, f_wq, f_subkey1, f_subkey2, f_u, f_v, final_norm_g):
    B, S, D = x.shape
    L = N_META + S
    Lp = -(-L // BLOCK) * BLOCK
    T = B * Lp
    meta = jnp.broadcast_to(meta_tokens.astype(x.dtype)[None], (B, N_META, D))
    h = jnp.concatenate([meta, x, jnp.zeros((B, Lp - L, D), x.dtype)], axis=1)
    cos_t, sin_t = _rope_tables(Lp)
    nk = 2 * N_HEADS * HEAD_DIM
    k3 = v3 = None
    for layer in range(DEPTH):
        if layer < N_A_LAYERS:
            i = layer
            h = _conformer_layer(h, a_norm_g[i], a_pw1_w[i], a_pw1_b[i], a_dw_w[i], a_dw_b[i],
                                 a_ln_g[i], a_ln_b[i], a_pw2_w[i], a_pw2_b[i])
        else:
            j = layer - N_A_LAYERS
            lam_init = 0.8 - 0.6 * math.exp(-0.3 * layer)
            h2 = h.reshape(T, D)
            q = _norm_proj(h2, b_norm_g[j], b_wq[j], cos_t, sin_t, rope=True,
                           scale=HEAD_DIM ** -0.5, lp=Lp)
            lam_rows = jnp.zeros((8, 128), F32).at[0:4, 0:HEAD_DIM].set(
                jnp.stack([b_lambda_q1[j], b_lambda_k1[j], b_lambda_q2[j], b_lambda_k2[j]]).astype(F32))
            o = _diff_attention(q.reshape(B, Lp, nk), k3, v3, lam_rows, b_subln_g[j], lam_init)
            h = _matmul_res(o.reshape(T, N_HEADS * V_DIM), b_wo[j], h2).reshape(B, Lp, D)
        h2 = _peer_layer(h.reshape(T, D), f_norm_g[layer], f_wq[layer], f_subkey1[layer],
                         f_subkey2[layer], f_u[layer], f_v[layer])
        h = h2.reshape(B, Lp, D)
        if layer == N_A_LAYERS - 1:
            k3 = _norm_proj(h2, kv_norm_g, w_kv[:, :nk], cos_t, sin_t, rope=True, scale=1.0,
                            lp=Lp).reshape(B, Lp, nk)
            v3 = _norm_proj(h2, kv_norm_g, w_kv[:, nk:], cos_t, sin_t, rope=False, scale=1.0,
                            lp=Lp).reshape(B, Lp, N_HEADS * V_DIM)
    out = _final_norm(h.reshape(T, D), final_norm_g).reshape(B, Lp, D)
    return out[:, N_META:N_META + S]
```

```python
import functools
import math

import jax
import jax.numpy as jnp
from jax import lax
from jax.experimental import pallas as pl
from jax.experimental.pallas import tpu as pltpu

D_MODEL = 1024
N_META = 16
BLOCK = 128
DEPTH = 4
N_A_LAYERS = DEPTH // 2
CONV_WIDTH = 31
N_HEADS = 8
HEAD_DIM = 64
V_DIM = 128
ROT_DIM = 16
ROPE_THETA = 500000.0
PEER_HEADS = 8
PEER_KEYS = 128
PEER_EXPERTS = PEER_KEYS * PEER_KEYS
PEER_TOPK = 16
EPS = 1e-6

NEG = -1e30
VMEM_LIMIT = 52 * 1024 * 1024

F32 = jnp.float32
BF16 = jnp.bfloat16


def _cparams(sem, flags=None):
    return pltpu.CompilerParams(dimension_semantics=sem, vmem_limit_bytes=VMEM_LIMIT, flags=flags)


def _rms(x, g):
    return x * lax.rsqrt(jnp.mean(x * x, axis=-1, keepdims=True) + EPS) * g


def _dot(a, b):
    return jnp.dot(a, b, preferred_element_type=F32)


def _dot_nt(a, b):
    return lax.dot_general(a, b, (((1,), (1,)), ((), ())), preferred_element_type=F32)


CONF_TS = 384
CONF_HALO = 32
CONF_RB = 32
CONF_CB = 256


def _conformer_kernel(x_ref, ng_ref, w1a_ref, w1g_ref, b1a_ref, b1g_ref, dww_ref, dwb_ref,
                      lng_ref, lnb_ref, w2_ref, b2_ref, o_ref, buf_ref, cbuf_ref):
    ts = CONF_TS

    @pl.when(pl.program_id(1) == 0)
    def _():
        buf_ref[0:CONF_HALO, :] = jnp.zeros((CONF_HALO, D_MODEL), F32)

    x = x_ref[0]
    n = _rms(x, ng_ref[...]).astype(BF16)
    a = _dot(n, w1a_ref[...]) + b1a_ref[...]
    g = _dot(n, w1g_ref[...]) + b1g_ref[...]
    buf_ref[CONF_HALO:CONF_HALO + ts, :] = a * jax.nn.sigmoid(g)

    off0 = CONF_HALO - (CONV_WIDTH - 1)

    def row_body(rb, carry):
        r0 = pl.multiple_of(rb * CONF_RB, CONF_RB)
        for cb in range(D_MODEL // CONF_CB):
            cols = slice(cb * CONF_CB, (cb + 1) * CONF_CB)
            acc = jnp.broadcast_to(dwb_ref[:, cols], (CONF_RB, CONF_CB))
            win = buf_ref[pl.ds(r0, CONF_RB + CONF_HALO), cols]
            for k in range(CONV_WIDTH):
                acc = acc + dww_ref[k:k + 1, cols] * win[off0 + k:off0 + k + CONF_RB]
            cbuf_ref[pl.ds(r0, CONF_RB), cols] = acc
        return carry

    lax.fori_loop(0, ts // CONF_RB, row_body, 0)
    buf_ref[0:CONF_HALO, :] = buf_ref[ts:ts + CONF_HALO, :]

    c = cbuf_ref[...]
    mu = jnp.mean(c, axis=-1, keepdims=True)
    cc = c - mu
    y = cc * lax.rsqrt(jnp.mean(cc * cc, axis=-1, keepdims=True) + EPS)
    y = y * lng_ref[...] + lnb_ref[...]
    y = (y * jax.nn.sigmoid(y)).astype(BF16)
    o_ref[0] = x + _dot(y, w2_ref[...]) + b2_ref[...]


def _conformer_layer(h3, ng, w1, b1, dww, dwb, lng, lnb, w2, b2):
    B, Lp, D = h3.shape
    ts = CONF_TS
    row = lambda v: v.reshape(1, -1).astype(F32)
    const = lambda shape: pl.BlockSpec(shape, lambda b, t: (0,) * len(shape))
    return pl.pallas_call(
        _conformer_kernel,
        grid=(B, Lp // ts),
        in_specs=[
            pl.BlockSpec((1, ts, D), lambda b, t: (b, t, 0)),
            const((1, D)), const((D, D)), const((D, D)), const((1, D)), const((1, D)),
            const((CONV_WIDTH, D)), const((1, D)), const((1, D)), const((1, D)),
            const((D, D)), const((1, D)),
        ],
        out_specs=pl.BlockSpec((1, ts, D), lambda b, t: (b, t, 0)),
        out_shape=jax.ShapeDtypeStruct((B, Lp, D), F32),
        scratch_shapes=[pltpu.VMEM((ts + CONF_HALO, D), F32), pltpu.VMEM((ts, D), F32)],
        compiler_params=_cparams(("arbitrary", "arbitrary")),
        name="conformer",
    )(h3, row(ng), w1[:, :D].astype(BF16), w1[:, D:].astype(BF16), row(b1[:D]), row(b1[D:]),
      dww.astype(F32), row(dwb), row(lng), row(lnb), w2.astype(BF16), row(b2))


PROJ_TS = 384


def _proj_kernel(x_ref, g_ref, w_ref, cos_ref, sin_ref, o_ref, *, rope, scale):
    n = _rms(x_ref[...], g_ref[...]).astype(BF16)
    y = _dot(n, w_ref[...])
    if rope:
        cos = cos_ref[...]
        sin = sin_ref[...]
        lane = lax.broadcasted_iota(jnp.int32, cos.shape, 1) % HEAD_DIM
        for cb in range(y.shape[1] // 128):
            yb = y[:, cb * 128:(cb + 1) * 128]
            fwd = pltpu.roll(yb, 128 - ROT_DIM // 2, axis=1)
            bwd = pltpu.roll(yb, ROT_DIM // 2, axis=1)
            partner = jnp.where(lane < ROT_DIM // 2, fwd, bwd)
            o_ref[:, cb * 128:(cb + 1) * 128] = ((yb * cos + partner * sin) * scale).astype(BF16)
    else:
        o_ref[...] = (y * scale).astype(BF16)


def _norm_proj(h2, g, w, cos_t, sin_t, *, rope, scale, lp):
    T, D = h2.shape
    N = w.shape[1]
    ts = PROJ_TS
    nt = lp // ts
    return pl.pallas_call(
        functools.partial(_proj_kernel, rope=rope, scale=scale),
        grid=(T // ts,),
        in_specs=[
            pl.BlockSpec((ts, D), lambda i: (i, 0)),
            pl.BlockSpec((1, D), lambda i: (0, 0)),
            pl.BlockSpec((D, N), lambda i: (0, 0)),
            pl.BlockSpec((ts, 128), lambda i: (i % nt, 0)),
            pl.BlockSpec((ts, 128), lambda i: (i % nt, 0)),
        ],
        out_specs=pl.BlockSpec((ts, N), lambda i: (i, 0)),
        out_shape=jax.ShapeDtypeStruct((T, N), BF16),
        compiler_params=_cparams(("arbitrary",)),
        name="norm_proj",
    )(h2, g.reshape(1, D).astype(F32), w.astype(BF16), cos_t, sin_t)


def _rope_tables(lp):
    inv = ROPE_THETA ** (-jnp.arange(0, ROT_DIM, 2, dtype=F32) / ROT_DIM)
    d = jnp.arange(128) % HEAD_DIM
    ang = jnp.arange(lp, dtype=F32)[:, None] * inv[d % (ROT_DIM // 2)][None, :]
    cos_t = jnp.where(d[None, :] < ROT_DIM, jnp.cos(ang), 1.0)
    sin_t = jnp.where(d[None, :] < ROT_DIM // 2, -jnp.sin(ang),
                      jnp.where(d[None, :] < ROT_DIM, jnp.sin(ang), 0.0))
    return cos_t.astype(F32), sin_t.astype(F32)


ATT_T = 384


def _attn_kernel(q_ref, k_ref, v_ref, lam_ref, sg_ref, o_ref, *, lam_init):
    tq = ATT_T
    qi = pl.program_id(2)
    q = q_ref[0]
    lane = lax.broadcasted_iota(jnp.int32, q.shape, 1)
    zero = jnp.zeros_like(q)
    qq = jnp.concatenate([jnp.where(lane < HEAD_DIM, q, zero),
                          jnp.where(lane >= HEAD_DIM, q, zero)], axis=0)

    def step(kc, carry, masked):
        m, l, acc = carry
        k0 = pl.multiple_of(kc * tq, tq)
        k = k_ref[0, pl.ds(k0, tq), :]
        v = v_ref[0, pl.ds(k0, tq), :]
        s = _dot_nt(qq, k)
        if masked:
            r = lax.broadcasted_iota(jnp.int32, s.shape, 0)
            r = jnp.where(r >= tq, r - tq, r)
            c = lax.broadcasted_iota(jnp.int32, s.shape, 1)
            s = jnp.where(c <= r, s, NEG)
        m_new = jnp.maximum(m, jnp.max(s, axis=-1, keepdims=True))
        alpha = jnp.exp(m - m_new)
        p = jnp.exp(s - m_new)
        l = alpha * l + jnp.sum(p, axis=-1, keepdims=True)
        acc = alpha * acc + _dot(p.astype(BF16), v)
        return m_new, l, acc

    init = (jnp.full((2 * tq, 1), NEG, F32), jnp.zeros((2 * tq, 1), F32),
            jnp.zeros((2 * tq, V_DIM), F32))
    carry = lax.fori_loop(0, qi, lambda kc, c: step(kc, c, False), init)
    m, l, acc = step(qi, carry, True)
    o = acc / l
    lp = lam_ref[...]
    lam = (jnp.exp(jnp.sum(lp[0:1] * lp[1:2], axis=-1, keepdims=True))
           - jnp.exp(jnp.sum(lp[2:3] * lp[3:4], axis=-1, keepdims=True)) + lam_init)
    od = o[:tq] - lam * o[tq:]
    od = od * lax.rsqrt(jnp.mean(od * od, axis=-1, keepdims=True) + EPS) * sg_ref[...]
    o_ref[0] = (od * (1.0 - lam_init)).astype(BF16)


def _diff_attention(q3, k3, v3, lam_rows, subln_g, lam_init):
    B, Lp, _ = q3.shape
    tq = ATT_T
    return pl.pallas_call(
        functools.partial(_attn_kernel, lam_init=lam_init),
        grid=(B, N_HEADS, Lp // tq),
        in_specs=[
            pl.BlockSpec((1, tq, 128), lambda b, h, i: (b, i, h)),
            pl.BlockSpec((1, Lp, 128), lambda b, h, i: (b, 0, h)),
            pl.BlockSpec((1, Lp, 128), lambda b, h, i: (b, 0, h)),
            pl.BlockSpec((8, 128), lambda b, h, i: (0, 0)),
            pl.BlockSpec((1, V_DIM), lambda b, h, i: (0, 0)),
        ],
        out_specs=pl.BlockSpec((1, tq, 128), lambda b, h, i: (b, i, h)),
        out_shape=jax.ShapeDtypeStruct((B, Lp, N_HEADS * V_DIM), BF16),
        compiler_params=_cparams(("arbitrary", "arbitrary", "arbitrary")),
        name="diff_attn",
    )(q3, k3, v3, lam_rows, subln_g.reshape(1, V_DIM).astype(F32))


def _matmul_res_kernel(a_ref, w_ref, h_ref, o_ref):
    o_ref[...] = h_ref[...] + _dot(a_ref[...], w_ref[...])


def _matmul_res(a2, w, h2):
    T, K = a2.shape
    N = w.shape[1]
    ts = PROJ_TS
    return pl.pallas_call(
        _matmul_res_kernel,
        grid=(T // ts,),
        in_specs=[pl.BlockSpec((ts, K), lambda i: (i, 0)),
                  pl.BlockSpec((K, N), lambda i: (0, 0)),
                  pl.BlockSpec((ts, N), lambda i: (i, 0))],
        out_specs=pl.BlockSpec((ts, N), lambda i: (i, 0)),
        out_shape=jax.ShapeDtypeStruct((T, N), F32),
        compiler_params=_cparams(("arbitrary",)),
        name="out_proj",
    )(a2, w.astype(BF16), h2)


ROUTE_TS = 256
RANK_SCALE = 2.0 ** 100
RANK_BIAS = 64.0


def _extract_top(s, n):
    rows = lax.broadcasted_iota(jnp.int32, (n, s.shape[1]), 0)
    vals = jnp.full((n, s.shape[1]), NEG, F32)
    work = s
    for j in range(n):
        m = jnp.max(work, axis=0, keepdims=True)
        vals = jnp.where(rows == j, m, vals)
        work = jnp.where(work >= m, -(RANK_BIAS + j) * RANK_SCALE, work)
    rank = jnp.where(work < -0.5 * RANK_BIAS * RANK_SCALE, work * (-1.0 / RANK_SCALE) - RANK_BIAS, float(n))
    return vals, rank


def _top_values(s, n):
    rows = lax.broadcasted_iota(jnp.int32, (n, s.shape[1]), 0)
    vals = jnp.full((n, s.shape[1]), NEG, F32)
    work = s
    for j in range(n):
        m = jnp.max(work, axis=0, keepdims=True)
        vals = jnp.where(rows == j, m, vals)
        if j + 1 < n:
            work = jnp.where(work >= m, NEG, work)
    return vals


def _candidate_sums(v1, v2):
    k = PEER_TOPK
    row8 = lax.broadcasted_iota(jnp.int32, (8, v1.shape[1]), 0)
    blocks = [v1[0:1] + v2]
    for a in range(1, 8):
        nb = k // (a + 1)
        blocks.append(jnp.where(row8 < nb, v1[a:a + 1] + v2[0:8], NEG))
    blocks.append(v1[8:16] + v2[0:1])
    return jnp.concatenate(blocks, axis=0)


def _route_kernel(h_ref, g_ref, wq_ref, sk1_ref, sk2_ref, xn_ref, c_ref, rk_ref, n_ref, r_ref):
    k = PEER_TOPK
    xn = _rms(h_ref[...], g_ref[...]).astype(BF16)
    xn_ref[...] = xn
    q = _dot(xn, wq_ref[...]).astype(BF16)
    for hd in range(PEER_HEADS):
        q1 = q[:, hd * 256:hd * 256 + 128]
        q2 = q[:, hd * 256 + 128:(hd + 1) * 256]
        s1 = _dot_nt(sk1_ref[...], q1)
        s2 = _dot_nt(sk2_ref[...], q2)
        v1, rank1 = _extract_top(s1, k)
        v2, rank2 = _extract_top(s2, k)
        tops = _top_values(_candidate_sums(v1, v2), k)
        z = jnp.sum(jnp.exp(tops - tops[0:1]), axis=0, keepdims=True)
        tau = tops[k - 1:k]
        cnt = jnp.zeros_like(v1)
        for b in range(k):
            cnt = cnt + jnp.where(v1 + v2[b:b + 1] >= tau, 1.0, 0.0)
        rank1_b = rank1.astype(BF16)
        nn = jnp.zeros(rank1_b.shape, BF16)
        for a in range(k):
            row = jnp.broadcast_to(cnt[a:a + 1].astype(BF16), nn.shape)
            nn = jnp.where(rank1_b == a, row, nn)
        n_ref[hd] = nn.astype(F32)
        r_ref[hd] = jnp.exp(s1 - v1[0:1]) / z
        rk_ref[hd] = rank2.astype(BF16)
        c_ref[hd] = jnp.exp(s2 - v2[0:1]).astype(BF16)


def _peer_route(h2, g, wq, sk1, sk2):
    T, D = h2.shape
    ts = ROUTE_TS
    nq = wq.shape[1]
    tab32 = jax.ShapeDtypeStruct((PEER_HEADS, PEER_KEYS, T), F32)
    tab16 = jax.ShapeDtypeStruct((PEER_HEADS, PEER_KEYS, T), BF16)
    tab_spec = pl.BlockSpec((PEER_HEADS, PEER_KEYS, ts), lambda i: (0, 0, i))
    return pl.pallas_call(
        _route_kernel,
        grid=(T // ts,),
        in_specs=[pl.BlockSpec((ts, D), lambda i: (i, 0)),
                  pl.BlockSpec((1, D), lambda i: (0, 0)),
                  pl.BlockSpec((D, nq), lambda i: (0, 0)),
                  pl.BlockSpec((PEER_KEYS, 128), lambda i: (0, 0)),
                  pl.BlockSpec((PEER_KEYS, 128), lambda i: (0, 0))],
        out_specs=[pl.BlockSpec((ts, D), lambda i: (i, 0)), tab_spec, tab_spec, tab_spec, tab_spec],
        out_shape=[jax.ShapeDtypeStruct((T, D), BF16), tab16, tab16, tab32, tab32],
        compiler_params=_cparams(("arbitrary",)),
        name="peer_route",
    )(h2, g.reshape(1, D).astype(F32), wq.astype(BF16), sk1.astype(BF16), sk2.astype(BF16))


PEER_TT = 512
PEER_TC = 256
PEER_ET = 1024
SQRT_HALF = math.sqrt(0.5)


def _peer_dense_kernel(xn_ref, u_ref, vt_ref, c_ref, rk_ref, n_ref, r_ref, h_ref, o_ref,
                       acc_ref, p_ref):
    j = pl.program_id(1)
    tiles_per_step = PEER_ET // PEER_KEYS

    @pl.when(j == 0)
    def _():
        acc_ref[...] = jnp.zeros_like(acc_ref)

    z = _dot_nt(u_ref[...], xn_ref[...])
    for ii in range(tiles_per_step):
        i1 = j * tiles_per_step + ii
        for tc in range(PEER_TT // PEER_TC):
            tok = slice(tc * PEER_TC, (tc + 1) * PEER_TC)
            nb, rb = [], []
            for hd in range(PEER_HEADS):
                nb.append(jnp.broadcast_to(n_ref[hd, pl.ds(i1, 1), tok], (16, PEER_TC)).astype(BF16))
                rb.append(jnp.broadcast_to(r_ref[hd, pl.ds(i1, 1), tok], (16, PEER_TC)).astype(BF16))
            for kb in range(PEER_KEYS // 16):
                rows = slice(kb * 16, (kb + 1) * 16)
                erow = slice(ii * PEER_KEYS + kb * 16, ii * PEER_KEYS + (kb + 1) * 16)
                w = jnp.zeros((16, PEER_TC), BF16)
                for hd in range(PEER_HEADS):
                    w = w + rb[hd] * jnp.where(rk_ref[hd, rows, tok] < nb[hd], c_ref[hd, rows, tok],
                                               jnp.zeros_like(w))
                zz = z[erow, tok]
                act = 0.5 * zz * (1.0 + lax.erf(zz * SQRT_HALF))
                p_ref[erow, tok] = w * act.astype(BF16)
    acc_ref[...] += _dot(vt_ref[...], p_ref[...])

    @pl.when(j == pl.num_programs(1) - 1)
    def _():
        o_ref[...] = h_ref[...] + acc_ref[...].T


def _peer_dense(xn, u_bf, vt_bf, c, rk, n, r, h2):
    T, D = h2.shape
    tt, et = PEER_TT, PEER_ET
    tab_spec = pl.BlockSpec((PEER_HEADS, PEER_KEYS, tt), lambda i, j: (0, 0, i))
    return pl.pallas_call(
        _peer_dense_kernel,
        grid=(T // tt, PEER_EXPERTS // et),
        in_specs=[pl.BlockSpec((tt, D), lambda i, j: (i, 0)),
                  pl.BlockSpec((et, D), lambda i, j: (j, 0)),
                  pl.BlockSpec((D, et), lambda i, j: (0, j)),
                  tab_spec, tab_spec, tab_spec, tab_spec,
                  pl.BlockSpec((tt, D), lambda i, j: (i, 0))],
        out_specs=pl.BlockSpec((tt, D), lambda i, j: (i, 0)),
        out_shape=jax.ShapeDtypeStruct((T, D), F32),
        scratch_shapes=[pltpu.VMEM((D, tt), F32), pltpu.VMEM((et, tt), BF16)],
        compiler_params=_cparams(("arbitrary", "arbitrary")),
        name="peer_dense",
    )(xn, u_bf, vt_bf, c, rk, n, r, h2)


def _peer_layer(h2, g, wq, sk1, sk2, u_tab, v_tab):
    xn, c, rk, n, r = _peer_route(h2, g, wq, sk1, sk2)
    return _peer_dense(xn, u_tab.astype(BF16), v_tab.astype(BF16).T, c, rk, n, r, h2)


def _final_norm_kernel(x_ref, g_ref, o_ref):
    o_ref[...] = _rms(x_ref[...], g_ref[...])


def _final_norm(h2, g):
    T, D = h2.shape
    ts = PROJ_TS
    return pl.pallas_call(
        _final_norm_kernel,
        grid=(T // ts,),
        in_specs=[pl.BlockSpec((ts, D), lambda i: (i, 0)), pl.BlockSpec((1, D), lambda i: (0, 0))],
        out_specs=pl.BlockSpec((ts, D), lambda i: (i, 0)),
        out_shape=jax.ShapeDtypeStruct((T, D), F32),
        compiler_params=_cparams(("arbitrary",)),
        name="final_norm",
    )(h2, g.reshape(1, D).astype(F32))


def kernel(x, meta_tokens, a_norm_g, a_pw1_w, a_pw1_b, a_dw_w, a_dw_b, a_ln_g, a_ln_b, a_pw2_w, a_pw2_b, kv_norm_g, w_kv, b_norm_g, b_wq, b_lambda_q1, b_lambda_k1, b_lambda_q2, b_lambda_k2, b_subln_g, b_wo, f_norm_g, f_wq, f_subkey1, f_subkey2, f_u, f_v, final_norm_g):
    B, S, D = x.shape
    L = N_META + S
    Lp = -(-L // BLOCK) * BLOCK
    T = B * Lp
    meta = jnp.broadcast_to(meta_tokens.astype(x.dtype)[None], (B, N_META, D))
    h = jnp.concatenate([meta, x, jnp.zeros((B, Lp - L, D), x.dtype)], axis=1)
    cos_t, sin_t = _rope_tables(Lp)
    nk = 2 * N_HEADS * HEAD_DIM
    k3 = v3 = None
    for layer in range(DEPTH):
        if layer < N_A_LAYERS:
            i = layer
            h = _conformer_layer(h, a_norm_g[i], a_pw1_w[i], a_pw1_b[i], a_dw_w[i], a_dw_b[i],
                                 a_ln_g[i], a_ln_b[i], a_pw2_w[i], a_pw2_b[i])
        else:
            j = layer - N_A_LAYERS
            lam_init = 0.8 - 0.6 * math.exp(-0.3 * layer)
            h2 = h.reshape(T, D)
            q = _norm_proj(h2, b_norm_g[j], b_wq[j], cos_t, sin_t, rope=True,
                           scale=HEAD_DIM ** -0.5, lp=Lp)
            lam_rows = jnp.zeros((8, 128), F32).at[0:4, 0:HEAD_DIM].set(
                jnp.stack([b_lambda_q1[j], b_lambda_k1[j], b_lambda_q2[j], b_lambda_k2[j]]).astype(F32))
            o = _diff_attention(q.reshape(B, Lp, nk), k3, v3, lam_rows, b_subln_g[j], lam_init)
            h = _matmul_res(o.reshape(T, N_HEADS * V_DIM), b_wo[j], h2).reshape(B, Lp, D)
        h2 = _peer_layer(h.reshape(T, D), f_norm_g[layer], f_wq[layer], f_subkey1[layer],
                         f_subkey2[layer], f_u[layer], f_v[layer])
        h = h2.reshape(B, Lp, D)
        if layer == N_A_LAYERS - 1:
            k3 = _norm_proj(h2, kv_norm_g, w_kv[:, :nk], cos_t, sin_t, rope=True, scale=1.0,
                            lp=Lp).reshape(B, Lp, nk)
            v3 = _norm_proj(h2, kv_norm_g, w_kv[:, nk:], cos_t, sin_t, rope=False, scale=1.0,
                            lp=Lp).reshape(B, Lp, N_HEADS * V_DIM)
    out = _final_norm(h.reshape(T, D), final_norm_g).reshape(B, Lp, D)
    return out[:, N_META:N_META + S]
```

```python
import functools
import math

import jax
import jax.numpy as jnp
from jax import lax
from jax.experimental import pallas as pl
from jax.experimental.pallas import tpu as pltpu

D_MODEL = 1024
N_META = 16
BLOCK = 128
DEPTH = 4
N_A_LAYERS = DEPTH // 2
CONV_WIDTH = 31
N_HEADS = 8
HEAD_DIM = 64
V_DIM = 128
ROT_DIM = 16
ROPE_THETA = 500000.0
PEER_HEADS = 8
PEER_KEYS = 128
PEER_EXPERTS = PEER_KEYS * PEER_KEYS
PEER_TOPK = 16
EPS = 1e-6

NEG = -1e30
VMEM_LIMIT = 52 * 1024 * 1024

F32 = jnp.float32
BF16 = jnp.bfloat16


def _cparams(sem, flags=None):
    return pltpu.CompilerParams(dimension_semantics=sem, vmem_limit_bytes=VMEM_LIMIT, flags=flags)


def _rms(x, g):
    return x * lax.rsqrt(jnp.mean(x * x, axis=-1, keepdims=True) + EPS) * g


def _dot(a, b):
    return jnp.dot(a, b, preferred_element_type=F32)


def _dot_nt(a, b):
    return lax.dot_general(a, b, (((1,), (1,)), ((), ())), preferred_element_type=F32)


CONF_TS = 384
CONF_HALO = 32
CONF_RB = 32
CONF_CB = 256


def _conformer_kernel(x_ref, ng_ref, w1a_ref, w1g_ref, b1a_ref, b1g_ref, dww_ref, dwb_ref,
                      lng_ref, lnb_ref, w2_ref, b2_ref, o_ref, buf_ref, cbuf_ref):
    ts = CONF_TS

    @pl.when(pl.program_id(1) == 0)
    def _():
        buf_ref[0:CONF_HALO, :] = jnp.zeros((CONF_HALO, D_MODEL), F32)

    x = x_ref[0]
    n = _rms(x, ng_ref[...]).astype(BF16)
    a = _dot(n, w1a_ref[...]) + b1a_ref[...]
    g = _dot(n, w1g_ref[...]) + b1g_ref[...]
    buf_ref[CONF_HALO:CONF_HALO + ts, :] = a * jax.nn.sigmoid(g)

    off0 = CONF_HALO - (CONV_WIDTH - 1)

    def row_body(rb, carry):
        r0 = pl.multiple_of(rb * CONF_RB, CONF_RB)
        for cb in range(D_MODEL // CONF_CB):
            cols = slice(cb * CONF_CB, (cb + 1) * CONF_CB)
            acc = jnp.broadcast_to(dwb_ref[:, cols], (CONF_RB, CONF_CB))
            win = buf_ref[pl.ds(r0, CONF_RB + CONF_HALO), cols]
            shifted = [win] + [pltpu.roll(win, CONF_RB + CONF_HALO - j, axis=0) for j in range(1, 8)]
            for k in range(CONV_WIDTH):
                q8, j = divmod(off0 + k, 8)
                acc = acc + dww_ref[k:k + 1, cols] * shifted[j][8 * q8:8 * q8 + CONF_RB]
            cbuf_ref[pl.ds(r0, CONF_RB), cols] = acc
        return carry

    lax.fori_loop(0, ts // CONF_RB, row_body, 0)
    buf_ref[0:CONF_HALO, :] = buf_ref[ts:ts + CONF_HALO, :]

    c = cbuf_ref[...]
    mu = jnp.mean(c, axis=-1, keepdims=True)
    cc = c - mu
    y = cc * lax.rsqrt(jnp.mean(cc * cc, axis=-1, keepdims=True) + EPS)
    y = y * lng_ref[...] + lnb_ref[...]
    y = (y * jax.nn.sigmoid(y)).astype(BF16)
    o_ref[0] = x + _dot(y, w2_ref[...]) + b2_ref[...]


def _conformer_layer(h3, ng, w1, b1, dww, dwb, lng, lnb, w2, b2):
    B, Lp, D = h3.shape
    ts = CONF_TS
    row = lambda v: v.reshape(1, -1).astype(F32)
    const = lambda shape: pl.BlockSpec(shape, lambda b, t: (0,) * len(shape))
    return pl.pallas_call(
        _conformer_kernel,
        grid=(B, Lp // ts),
        in_specs=[
            pl.BlockSpec((1, ts, D), lambda b, t: (b, t, 0)),
            const((1, D)), const((D, D)), const((D, D)), const((1, D)), const((1, D)),
            const((CONV_WIDTH, D)), const((1, D)), const((1, D)), const((1, D)),
            const((D, D)), const((1, D)),
        ],
        out_specs=pl.BlockSpec((1, ts, D), lambda b, t: (b, t, 0)),
        out_shape=jax.ShapeDtypeStruct((B, Lp, D), F32),
        scratch_shapes=[pltpu.VMEM((ts + CONF_HALO, D), F32), pltpu.VMEM((ts, D), F32)],
        compiler_params=_cparams(("arbitrary", "arbitrary")),
        name="conformer",
    )(h3, row(ng), w1[:, :D].astype(BF16), w1[:, D:].astype(BF16), row(b1[:D]), row(b1[D:]),
      dww.astype(F32), row(dwb), row(lng), row(lnb), w2.astype(BF16), row(b2))


PROJ_TS = 384


def _proj_kernel(x_ref, g_ref, w_ref, cos_ref, sin_ref, o_ref, *, rope, scale):
    n = _rms(x_ref[...], g_ref[...]).astype(BF16)
    y = _dot(n, w_ref[...])
    if rope:
        cos = cos_ref[...]
        sin = sin_ref[...]
        lane = lax.broadcasted_iota(jnp.int32, cos.shape, 1) % HEAD_DIM
        for cb in range(y.shape[1] // 128):
            yb = y[:, cb * 128:(cb + 1) * 128]
            fwd = pltpu.roll(yb, 128 - ROT_DIM // 2, axis=1)
            bwd = pltpu.roll(yb, ROT_DIM // 2, axis=1)
            partner = jnp.where(lane < ROT_DIM // 2, fwd, bwd)
            o_ref[:, cb * 128:(cb + 1) * 128] = ((yb * cos + partner * sin) * scale).astype(BF16)
    else:
        o_ref[...] = (y * scale).astype(BF16)


def _norm_proj(h2, g, w, cos_t, sin_t, *, rope, scale, lp):
    T, D = h2.shape
    N = w.shape[1]
    ts = PROJ_TS
    nt = lp // ts
    return pl.pallas_call(
        functools.partial(_proj_kernel, rope=rope, scale=scale),
        grid=(T // ts,),
        in_specs=[
            pl.BlockSpec((ts, D), lambda i: (i, 0)),
            pl.BlockSpec((1, D), lambda i: (0, 0)),
            pl.BlockSpec((D, N), lambda i: (0, 0)),
            pl.BlockSpec((ts, 128), lambda i: (i % nt, 0)),
            pl.BlockSpec((ts, 128), lambda i: (i % nt, 0)),
        ],
        out_specs=pl.BlockSpec((ts, N), lambda i: (i, 0)),
        out_shape=jax.ShapeDtypeStruct((T, N), BF16),
        compiler_params=_cparams(("arbitrary",)),
        name="norm_proj",
    )(h2, g.reshape(1, D).astype(F32), w.astype(BF16), cos_t, sin_t)


def _rope_tables(lp):
    inv = ROPE_THETA ** (-jnp.arange(0, ROT_DIM, 2, dtype=F32) / ROT_DIM)
    d = jnp.arange(128) % HEAD_DIM
    ang = jnp.arange(lp, dtype=F32)[:, None] * inv[d % (ROT_DIM // 2)][None, :]
    cos_t = jnp.where(d[None, :] < ROT_DIM, jnp.cos(ang), 1.0)
    sin_t = jnp.where(d[None, :] < ROT_DIM // 2, -jnp.sin(ang),
                      jnp.where(d[None, :] < ROT_DIM, jnp.sin(ang), 0.0))
    return cos_t.astype(F32), sin_t.astype(F32)


ATT_T = 384


def _attn_kernel(q_ref, k_ref, v_ref, lam_ref, sg_ref, o_ref, *, lam_init):
    tq = ATT_T
    qi = pl.program_id(2)
    q = q_ref[0]
    lane = lax.broadcasted_iota(jnp.int32, q.shape, 1)
    zero = jnp.zeros_like(q)
    qq = jnp.concatenate([jnp.where(lane < HEAD_DIM, q, zero),
                          jnp.where(lane >= HEAD_DIM, q, zero)], axis=0)

    def step(kc, carry, masked):
        m, l, acc = carry
        k0 = pl.multiple_of(kc * tq, tq)
        k = k_ref[0, pl.ds(k0, tq), :]
        v = v_ref[0, pl.ds(k0, tq), :]
        s = _dot_nt(qq, k)
        if masked:
            r = lax.broadcasted_iota(jnp.int32, s.shape, 0)
            r = jnp.where(r >= tq, r - tq, r)
            c = lax.broadcasted_iota(jnp.int32, s.shape, 1)
            s = jnp.where(c <= r, s, NEG)
        m_new = jnp.maximum(m, jnp.max(s, axis=-1, keepdims=True))
        alpha = jnp.exp(m - m_new)
        p = jnp.exp(s - m_new)
        l = alpha * l + jnp.sum(p, axis=-1, keepdims=True)
        acc = alpha * acc + _dot(p.astype(BF16), v)
        return m_new, l, acc

    init = (jnp.full((2 * tq, 1), NEG, F32), jnp.zeros((2 * tq, 1), F32),
            jnp.zeros((2 * tq, V_DIM), F32))
    carry = lax.fori_loop(0, qi, lambda kc, c: step(kc, c, False), init)
    m, l, acc = step(qi, carry, True)
    o = acc / l
    lp = lam_ref[...]
    lam = (jnp.exp(jnp.sum(lp[0:1] * lp[1:2], axis=-1, keepdims=True))
           - jnp.exp(jnp.sum(lp[2:3] * lp[3:4], axis=-1, keepdims=True)) + lam_init)
    od = o[:tq] - lam * o[tq:]
    od = od * lax.rsqrt(jnp.mean(od * od, axis=-1, keepdims=True) + EPS) * sg_ref[...]
    o_ref[0] = (od * (1.0 - lam_init)).astype(BF16)


def _diff_attention(q3, k3, v3, lam_rows, subln_g, lam_init):
    B, Lp, _ = q3.shape
    tq = ATT_T
    return pl.pallas_call(
        functools.partial(_attn_kernel, lam_init=lam_init),
        grid=(B, N_HEADS, Lp // tq),
        in_specs=[
            pl.BlockSpec((1, tq, 128), lambda b, h, i: (b, i, h)),
            pl.BlockSpec((1, Lp, 128), lambda b, h, i: (b, 0, h)),
            pl.BlockSpec((1, Lp, 128), lambda b, h, i: (b, 0, h)),
            pl.BlockSpec((8, 128), lambda b, h, i: (0, 0)),
            pl.BlockSpec((1, V_DIM), lambda b, h, i: (0, 0)),
        ],
        out_specs=pl.BlockSpec((1, tq, 128), lambda b, h, i: (b, i, h)),
        out_shape=jax.ShapeDtypeStruct((B, Lp, N_HEADS * V_DIM), BF16),
        compiler_params=_cparams(("arbitrary", "arbitrary", "arbitrary")),
        name="diff_attn",
    )(q3, k3, v3, lam_rows, subln_g.reshape(1, V_DIM).astype(F32))


def _matmul_res_kernel(a_ref, w_ref, h_ref, o_ref):
    o_ref[...] = h_ref[...] + _dot(a_ref[...], w_ref[...])


def _matmul_res(a2, w, h2):
    T, K = a2.shape
    N = w.shape[1]
    ts = PROJ_TS
    return pl.pallas_call(
        _matmul_res_kernel,
        grid=(T // ts,),
        in_specs=[pl.BlockSpec((ts, K), lambda i: (i, 0)),
                  pl.BlockSpec((K, N), lambda i: (0, 0)),
                  pl.BlockSpec((ts, N), lambda i: (i, 0))],
        out_specs=pl.BlockSpec((ts, N), lambda i: (i, 0)),
        out_shape=jax.ShapeDtypeStruct((T, N), F32),
        compiler_params=_cparams(("arbitrary",)),
        name="out_proj",
    )(a2, w.astype(BF16), h2)


ROUTE_TS = 256
RANK_SCALE = 2.0 ** 100
RANK_BIAS = 64.0


def _extract_top(s, n):
    rows = lax.broadcasted_iota(jnp.int32, (n, s.shape[1]), 0)
    vals = jnp.full((n, s.shape[1]), NEG, F32)
    work = s
    for j in range(n):
        m = jnp.max(work, axis=0, keepdims=True)
        vals = jnp.where(rows == j, m, vals)
        work = jnp.where(work >= m, -(RANK_BIAS + j) * RANK_SCALE, work)
    rank = jnp.where(work < -0.5 * RANK_BIAS * RANK_SCALE, work * (-1.0 / RANK_SCALE) - RANK_BIAS, float(n))
    return vals, rank


def _top_values(s, n):
    rows = lax.broadcasted_iota(jnp.int32, (n, s.shape[1]), 0)
    vals = jnp.full((n, s.shape[1]), NEG, F32)
    work = s
    for j in range(n):
        m = jnp.max(work, axis=0, keepdims=True)
        vals = jnp.where(rows == j, m, vals)
        if j + 1 < n:
            work = jnp.where(work >= m, NEG, work)
    return vals


def _candidate_sums(v1, v2):
    k = PEER_TOPK
    row8 = lax.broadcasted_iota(jnp.int32, (8, v1.shape[1]), 0)
    blocks = [v1[0:1] + v2]
    for a in range(1, 8):
        nb = k // (a + 1)
        blocks.append(jnp.where(row8 < nb, v1[a:a + 1] + v2[0:8], NEG))
    blocks.append(v1[8:16] + v2[0:1])
    return jnp.concatenate(blocks, axis=0)


def _route_kernel(h_ref, g_ref, wq_ref, sk1_ref, sk2_ref, xn_ref, c_ref, rk_ref, n_ref, r_ref):
    k = PEER_TOPK
    xn = _rms(h_ref[...], g_ref[...]).astype(BF16)
    xn_ref[...] = xn
    q = _dot(xn, wq_ref[...]).astype(BF16)
    for hd in range(PEER_HEADS):
        q1 = q[:, hd * 256:hd * 256 + 128]
        q2 = q[:, hd * 256 + 128:(hd + 1) * 256]
        s1 = _dot_nt(sk1_ref[...], q1)
        s2 = _dot_nt(sk2_ref[...], q2)
        v1, rank1 = _extract_top(s1, k)
        v2, rank2 = _extract_top(s2, k)
        tops = _top_values(_candidate_sums(v1, v2), k)
        z = jnp.sum(jnp.exp(tops - tops[0:1]), axis=0, keepdims=True)
        tau = tops[k - 1:k]
        cnt = jnp.zeros_like(v1)
        for b in range(k):
            cnt = cnt + jnp.where(v1 + v2[b:b + 1] >= tau, 1.0, 0.0)
        rank1_b = rank1.astype(BF16)
        nn = jnp.zeros(rank1_b.shape, BF16)
        for a in range(k):
            row = jnp.broadcast_to(cnt[a:a + 1].astype(BF16), nn.shape)
            nn = jnp.where(rank1_b == a, row, nn)
        n_ref[hd] = nn.astype(F32)
        r_ref[hd] = jnp.exp(s1 - v1[0:1]) / z
        rk_ref[hd] = rank2.astype(BF16)
        c_ref[hd] = jnp.exp(s2 - v2[0:1]).astype(BF16)


def _peer_route(h2, g, wq, sk1, sk2):
    T, D = h2.shape
    ts = ROUTE_TS
    nq = wq.shape[1]
    tab32 = jax.ShapeDtypeStruct((PEER_HEADS, PEER_KEYS, T), F32)
    tab16 = jax.ShapeDtypeStruct((PEER_HEADS, PEER_KEYS, T), BF16)
    tab_spec = pl.BlockSpec((PEER_HEADS, PEER_KEYS, ts), lambda i: (0, 0, i))
    return pl.pallas_call(
        _route_kernel,
        grid=(T // ts,),
        in_specs=[pl.BlockSpec((ts, D), lambda i: (i, 0)),
                  pl.BlockSpec((1, D), lambda i: (0, 0)),
                  pl.BlockSpec((D, nq), lambda i: (0, 0)),
                  pl.BlockSpec((PEER_KEYS, 128), lambda i: (0, 0)),
                  pl.BlockSpec((PEER_KEYS, 128), lambda i: (0, 0))],
        out_specs=[pl.BlockSpec((ts, D), lambda i: (i, 0)), tab_spec, tab_spec, tab_spec, tab_spec],
        out_shape=[jax.ShapeDtypeStruct((T, D), BF16), tab16, tab16, tab32, tab32],
        compiler_params=_cparams(("arbitrary",)),
        name="peer_route",
    )(h2, g.reshape(1, D).astype(F32), wq.astype(BF16), sk1.astype(BF16), sk2.astype(BF16))


PEER_TT = 512
PEER_TC = 256
PEER_ET = 2048
SQRT_HALF = math.sqrt(0.5)


def _peer_dense_kernel(xn_ref, u_ref, vt_ref, c_ref, rk_ref, n_ref, r_ref, h_ref, o_ref,
                       acc_ref, p_ref):
    j = pl.program_id(1)
    tiles_per_step = PEER_ET // PEER_KEYS

    @pl.when(j == 0)
    def _():
        acc_ref[...] = jnp.zeros_like(acc_ref)

    z = _dot_nt(u_ref[...], xn_ref[...])
    for ii in range(tiles_per_step):
        i1 = j * tiles_per_step + ii
        for tc in range(PEER_TT // PEER_TC):
            tok = slice(tc * PEER_TC, (tc + 1) * PEER_TC)
            nb, rb = [], []
            for hd in range(PEER_HEADS):
                nb.append(jnp.broadcast_to(n_ref[hd, pl.ds(i1, 1), tok], (16, PEER_TC)).astype(BF16))
                rb.append(jnp.broadcast_to(r_ref[hd, pl.ds(i1, 1), tok], (16, PEER_TC)).astype(BF16))
            for kb in range(PEER_KEYS // 16):
                rows = slice(kb * 16, (kb + 1) * 16)
                erow = slice(ii * PEER_KEYS + kb * 16, ii * PEER_KEYS + (kb + 1) * 16)
                w = jnp.zeros((16, PEER_TC), BF16)
                for hd in range(PEER_HEADS):
                    w = w + rb[hd] * jnp.where(rk_ref[hd, rows, tok] < nb[hd], c_ref[hd, rows, tok],
                                               jnp.zeros_like(w))
                zz = z[erow, tok]
                act = 0.5 * zz * (1.0 + lax.erf(zz * SQRT_HALF))
                p_ref[erow, tok] = w * act.astype(BF16)
    acc_ref[...] += _dot(vt_ref[...], p_ref[...])

    @pl.when(j == pl.num_programs(1) - 1)
    def _():
        o_ref[...] = h_ref[...] + acc_ref[...].T


def _peer_dense(xn, u_bf, vt_bf, c, rk, n, r, h2):
    T, D = h2.shape
    tt, et = PEER_TT, PEER_ET
    tab_spec = pl.BlockSpec((PEER_HEADS, PEER_KEYS, tt), lambda i, j: (0, 0, i))
    return pl.pallas_call(
        _peer_dense_kernel,
        grid=(T // tt, PEER_EXPERTS // et),
        in_specs=[pl.BlockSpec((tt, D), lambda i, j: (i, 0)),
                  pl.BlockSpec((et, D), lambda i, j: (j, 0)),
                  pl.BlockSpec((D, et), lambda i, j: (0, j)),
                  tab_spec, tab_spec, tab_spec, tab_spec,
                  pl.BlockSpec((tt, D), lambda i, j: (i, 0))],
        out_specs=pl.BlockSpec((tt, D), lambda i, j: (i, 0)),
        out_shape=jax.ShapeDtypeStruct((T, D), F32),
        scratch_shapes=[pltpu.VMEM((D, tt), F32), pltpu.VMEM((et, tt), BF16)],
        compiler_params=_cparams(("arbitrary", "arbitrary")),
        name="peer_dense",
    )(xn, u_bf, vt_bf, c, rk, n, r, h2)


def _peer_layer(h2, g, wq, sk1, sk2, u_tab, v_tab):
    xn, c, rk, n, r = _peer_route(h2, g, wq, sk1, sk2)
    return _peer_dense(xn, u_tab.astype(BF16), v_tab.astype(BF16).T, c, rk, n, r, h2)


def _final_norm_kernel(x_ref, g_ref, o_ref):
    o_ref[...] = _rms(x_ref[...], g_ref[...])


def _final_norm(h2, g):
    T, D = h2.shape
    ts = PROJ_TS
    return pl.pallas_call(
        _final_norm_kernel,
        grid=(T // ts,),
        in_specs=[pl.BlockSpec((ts, D), lambda i: (i, 0)), pl.BlockSpec((1, D), lambda i: (0, 0))],
        out_specs=pl.BlockSpec((ts, D), lambda i: (i, 0)),
        out_shape=jax.ShapeDtypeStruct((T, D), F32),
        compiler_params=_cparams(("arbitrary",)),
        name="final_norm",
    )(h2, g.reshape(1, D).astype(F32))


def kernel(x, meta_tokens, a_norm_g, a_pw1_w, a_pw1_b, a_dw_w, a_dw_b, a_ln_g, a_ln_b, a_pw2_w, a_pw2_b, kv_norm_g, w_kv, b_norm_g, b_wq, b_lambda_q1, b_lambda_k1, b_lambda_q2, b_lambda_k2, b_subln_g, b_wo, f_norm_g, f_wq, f_subkey1, f_subkey2, f_u, f_v, final_norm_g):
    B, S, D = x.shape
    L = N_META + S
    Lp = -(-L // BLOCK) * BLOCK
    T = B * Lp
    meta = jnp.broadcast_to(meta_tokens.astype(x.dtype)[None], (B, N_META, D))
    h = jnp.concatenate([meta, x, jnp.zeros((B, Lp - L, D), x.dtype)], axis=1)
    cos_t, sin_t = _rope_tables(Lp)
    nk = 2 * N_HEADS * HEAD_DIM
    k3 = v3 = None
    for layer in range(DEPTH):
        if layer < N_A_LAYERS:
            i = layer
            h = _conformer_layer(h, a_norm_g[i], a_pw1_w[i], a_pw1_b[i], a_dw_w[i], a_dw_b[i],
                                 a_ln_g[i], a_ln_b[i], a_pw2_w[i], a_pw2_b[i])
        else:
            j = layer - N_A_LAYERS
            lam_init = 0.8 - 0.6 * math.exp(-0.3 * layer)
            h2 = h.reshape(T, D)
            q = _norm_proj(h2, b_norm_g[j], b_wq[j], cos_t, sin_t, rope=True,
                           scale=HEAD_DIM ** -0.5, lp=Lp)
            lam_rows = jnp.zeros((8, 128), F32).at[0:4, 0:HEAD_DIM].set(
                jnp.stack([b_lambda_q1[j], b_lambda_k1[j], b_lambda_q2[j], b_lambda_k2[j]]).astype(F32))
            o = _diff_attention(q.reshape(B, Lp, nk), k3, v3, lam_rows, b_subln_g[j], lam_init)
            h = _matmul_res(o.reshape(T, N_HEADS * V_DIM), b_wo[j], h2).reshape(B, Lp, D)
        h2 = _peer_layer(h.reshape(T, D), f_norm_g[layer], f_wq[layer], f_subkey1[layer],
                         f_subkey2[layer], f_u[layer], f_v[layer])
        h = h2.reshape(B, Lp, D)
        if layer == N_A_LAYERS - 1:
            k3 = _norm_proj(h2, kv_norm_g, w_kv[:, :nk], cos_t, sin_t, rope=True, scale=1.0,
                            lp=Lp).reshape(B, Lp, nk)
            v3 = _norm_proj(h2, kv_norm_g, w_kv[:, nk:], cos_t, sin_t, rope=False, scale=1.0,
                            lp=Lp).reshape(B, Lp, N_HEADS * V_DIM)
    out = _final_norm(h.reshape(T, D), final_norm_g).reshape(B, Lp, D)
    return out[:, N_META:N_META + S]
```

```python
import functools
import math

import jax
import jax.numpy as jnp
from jax import lax
from jax.experimental import pallas as pl
from jax.experimental.pallas import tpu as pltpu

D_MODEL = 1024
N_META = 16
BLOCK = 128
DEPTH = 4
N_A_LAYERS = DEPTH // 2
CONV_WIDTH = 31
N_HEADS = 8
HEAD_DIM = 64
V_DIM = 128
ROT_DIM = 16
ROPE_THETA = 500000.0
PEER_HEADS = 8
PEER_KEYS = 128
PEER_EXPERTS = PEER_KEYS * PEER_KEYS
PEER_TOPK = 16
EPS = 1e-6

NEG = -1e30
VMEM_LIMIT = 52 * 1024 * 1024

F32 = jnp.float32
BF16 = jnp.bfloat16


def _cparams(sem, flags=None):
    return pltpu.CompilerParams(dimension_semantics=sem, vmem_limit_bytes=VMEM_LIMIT, flags=flags)


def _rms(x, g):
    return x * lax.rsqrt(jnp.mean(x * x, axis=-1, keepdims=True) + EPS) * g


def _dot(a, b):
    return jnp.dot(a, b, preferred_element_type=F32)


def _dot_nt(a, b):
    return lax.dot_general(a, b, (((1,), (1,)), ((), ())), preferred_element_type=F32)


CONF_TS = 384
CONF_HALO = 32
CONF_RB = 32
CONF_CB = 256


def _conformer_kernel(x_ref, ng_ref, w1a_ref, w1g_ref, b1a_ref, b1g_ref, dww_ref, dwb_ref,
                      lng_ref, lnb_ref, w2_ref, b2_ref, o_ref, buf_ref, cbuf_ref):
    ts = CONF_TS

    @pl.when(pl.program_id(1) == 0)
    def _():
        buf_ref[0:CONF_HALO, :] = jnp.zeros((CONF_HALO, D_MODEL), F32)

    x = x_ref[0]
    n = _rms(x, ng_ref[...]).astype(BF16)
    a = _dot(n, w1a_ref[...]) + b1a_ref[...]
    g = _dot(n, w1g_ref[...]) + b1g_ref[...]
    buf_ref[CONF_HALO:CONF_HALO + ts, :] = a * jax.nn.sigmoid(g)

    off0 = CONF_HALO - (CONV_WIDTH - 1)

    def row_body(rb, carry):
        r0 = pl.multiple_of(rb * CONF_RB, CONF_RB)
        for cb in range(D_MODEL // CONF_CB):
            cols = slice(cb * CONF_CB, (cb + 1) * CONF_CB)
            acc = jnp.broadcast_to(dwb_ref[:, cols], (CONF_RB, CONF_CB))
            win = buf_ref[pl.ds(r0, CONF_RB + CONF_HALO), cols]
            shifted = [win] + [pltpu.roll(win, CONF_RB + CONF_HALO - j, axis=0) for j in range(1, 8)]
            for k in range(CONV_WIDTH):
                q8, j = divmod(off0 + k, 8)
                acc = acc + dww_ref[k:k + 1, cols] * shifted[j][8 * q8:8 * q8 + CONF_RB]
            cbuf_ref[pl.ds(r0, CONF_RB), cols] = acc
        return carry

    lax.fori_loop(0, ts // CONF_RB, row_body, 0)
    buf_ref[0:CONF_HALO, :] = buf_ref[ts:ts + CONF_HALO, :]

    c = cbuf_ref[...]
    mu = jnp.mean(c, axis=-1, keepdims=True)
    cc = c - mu
    y = cc * lax.rsqrt(jnp.mean(cc * cc, axis=-1, keepdims=True) + EPS)
    y = y * lng_ref[...] + lnb_ref[...]
    y = (y * jax.nn.sigmoid(y)).astype(BF16)
    o_ref[0] = x + _dot(y, w2_ref[...]) + b2_ref[...]


def _conformer_layer(h3, ng, w1, b1, dww, dwb, lng, lnb, w2, b2):
    B, Lp, D = h3.shape
    ts = CONF_TS
    row = lambda v: v.reshape(1, -1).astype(F32)
    const = lambda shape: pl.BlockSpec(shape, lambda b, t: (0,) * len(shape))
    return pl.pallas_call(
        _conformer_kernel,
        grid=(B, Lp // ts),
        in_specs=[
            pl.BlockSpec((1, ts, D), lambda b, t: (b, t, 0)),
            const((1, D)), const((D, D)), const((D, D)), const((1, D)), const((1, D)),
            const((CONV_WIDTH, D)), const((1, D)), const((1, D)), const((1, D)),
            const((D, D)), const((1, D)),
        ],
        out_specs=pl.BlockSpec((1, ts, D), lambda b, t: (b, t, 0)),
        out_shape=jax.ShapeDtypeStruct((B, Lp, D), F32),
        scratch_shapes=[pltpu.VMEM((ts + CONF_HALO, D), F32), pltpu.VMEM((ts, D), F32)],
        compiler_params=_cparams(("arbitrary", "arbitrary")),
        name="conformer",
    )(h3, row(ng), w1[:, :D].astype(BF16), w1[:, D:].astype(BF16), row(b1[:D]), row(b1[D:]),
      dww.astype(F32), row(dwb), row(lng), row(lnb), w2.astype(BF16), row(b2))


PROJ_TS = 384


def _proj_kernel(x_ref, g_ref, w_ref, cos_ref, sin_ref, o_ref, *, rope, scale):
    n = _rms(x_ref[...], g_ref[...]).astype(BF16)
    y = _dot(n, w_ref[...])
    if rope:
        cos = cos_ref[...]
        sin = sin_ref[...]
        lane = lax.broadcasted_iota(jnp.int32, cos.shape, 1) % HEAD_DIM
        for cb in range(y.shape[1] // 128):
            yb = y[:, cb * 128:(cb + 1) * 128]
            fwd = pltpu.roll(yb, 128 - ROT_DIM // 2, axis=1)
            bwd = pltpu.roll(yb, ROT_DIM // 2, axis=1)
            partner = jnp.where(lane < ROT_DIM // 2, fwd, bwd)
            o_ref[:, cb * 128:(cb + 1) * 128] = ((yb * cos + partner * sin) * scale).astype(BF16)
    else:
        o_ref[...] = (y * scale).astype(BF16)


def _norm_proj(h2, g, w, cos_t, sin_t, *, rope, scale, lp):
    T, D = h2.shape
    N = w.shape[1]
    ts = PROJ_TS
    nt = lp // ts
    return pl.pallas_call(
        functools.partial(_proj_kernel, rope=rope, scale=scale),
        grid=(T // ts,),
        in_specs=[
            pl.BlockSpec((ts, D), lambda i: (i, 0)),
            pl.BlockSpec((1, D), lambda i: (0, 0)),
            pl.BlockSpec((D, N), lambda i: (0, 0)),
            pl.BlockSpec((ts, 128), lambda i: (i % nt, 0)),
            pl.BlockSpec((ts, 128), lambda i: (i % nt, 0)),
        ],
        out_specs=pl.BlockSpec((ts, N), lambda i: (i, 0)),
        out_shape=jax.ShapeDtypeStruct((T, N), BF16),
        compiler_params=_cparams(("arbitrary",)),
        name="norm_proj",
    )(h2, g.reshape(1, D).astype(F32), w.astype(BF16), cos_t, sin_t)


def _rope_tables(lp):
    inv = ROPE_THETA ** (-jnp.arange(0, ROT_DIM, 2, dtype=F32) / ROT_DIM)
    d = jnp.arange(128) % HEAD_DIM
    ang = jnp.arange(lp, dtype=F32)[:, None] * inv[d % (ROT_DIM // 2)][None, :]
    cos_t = jnp.where(d[None, :] < ROT_DIM, jnp.cos(ang), 1.0)
    sin_t = jnp.where(d[None, :] < ROT_DIM // 2, -jnp.sin(ang),
                      jnp.where(d[None, :] < ROT_DIM, jnp.sin(ang), 0.0))
    return cos_t.astype(F32), sin_t.astype(F32)


ATT_T = 384


def _attn_kernel(q_ref, k_ref, v_ref, lam_ref, sg_ref, o_ref, *, lam_init):
    tq = ATT_T
    qi = pl.program_id(2)
    q = q_ref[0]
    lane = lax.broadcasted_iota(jnp.int32, q.shape, 1)
    zero = jnp.zeros_like(q)
    qq = jnp.concatenate([jnp.where(lane < HEAD_DIM, q, zero),
                          jnp.where(lane >= HEAD_DIM, q, zero)], axis=0)

    def step(kc, carry, masked):
        m, l, acc = carry
        k0 = pl.multiple_of(kc * tq, tq)
        k = k_ref[0, pl.ds(k0, tq), :]
        v = v_ref[0, pl.ds(k0, tq), :]
        s = _dot_nt(qq, k)
        if masked:
            r = lax.broadcasted_iota(jnp.int32, s.shape, 0)
            r = jnp.where(r >= tq, r - tq, r)
            c = lax.broadcasted_iota(jnp.int32, s.shape, 1)
            s = jnp.where(c <= r, s, NEG)
        m_new = jnp.maximum(m, jnp.max(s, axis=-1, keepdims=True))
        alpha = jnp.exp(m - m_new)
        p = jnp.exp(s - m_new)
        l = alpha * l + jnp.sum(p, axis=-1, keepdims=True)
        acc = alpha * acc + _dot(p.astype(BF16), v)
        return m_new, l, acc

    init = (jnp.full((2 * tq, 1), NEG, F32), jnp.zeros((2 * tq, 1), F32),
            jnp.zeros((2 * tq, V_DIM), F32))
    carry = lax.fori_loop(0, qi, lambda kc, c: step(kc, c, False), init)
    m, l, acc = step(qi, carry, True)
    o = acc / l
    lp = lam_ref[...]
    lam = (jnp.exp(jnp.sum(lp[0:1] * lp[1:2], axis=-1, keepdims=True))
           - jnp.exp(jnp.sum(lp[2:3] * lp[3:4], axis=-1, keepdims=True)) + lam_init)
    od = o[:tq] - lam * o[tq:]
    od = od * lax.rsqrt(jnp.mean(od * od, axis=-1, keepdims=True) + EPS) * sg_ref[...]
    o_ref[0] = (od * (1.0 - lam_init)).astype(BF16)


def _diff_attention(q3, k3, v3, lam_rows, subln_g, lam_init):
    B, Lp, _ = q3.shape
    tq = ATT_T
    return pl.pallas_call(
        functools.partial(_attn_kernel, lam_init=lam_init),
        grid=(B, N_HEADS, Lp // tq),
        in_specs=[
            pl.BlockSpec((1, tq, 128), lambda b, h, i: (b, i, h)),
            pl.BlockSpec((1, Lp, 128), lambda b, h, i: (b, 0, h)),
            pl.BlockSpec((1, Lp, 128), lambda b, h, i: (b, 0, h)),
            pl.BlockSpec((8, 128), lambda b, h, i: (0, 0)),
            pl.BlockSpec((1, V_DIM), lambda b, h, i: (0, 0)),
        ],
        out_specs=pl.BlockSpec((1, tq, 128), lambda b, h, i: (b, i, h)),
        out_shape=jax.ShapeDtypeStruct((B, Lp, N_HEADS * V_DIM), BF16),
        compiler_params=_cparams(("arbitrary", "arbitrary", "arbitrary")),
        name="diff_attn",
    )(q3, k3, v3, lam_rows, subln_g.reshape(1, V_DIM).astype(F32))


def _matmul_res_kernel(a_ref, w_ref, h_ref, o_ref):
    o_ref[...] = h_ref[...] + _dot(a_ref[...], w_ref[...])


def _matmul_res(a2, w, h2):
    T, K = a2.shape
    N = w.shape[1]
    ts = PROJ_TS
    return pl.pallas_call(
        _matmul_res_kernel,
        grid=(T // ts,),
        in_specs=[pl.BlockSpec((ts, K), lambda i: (i, 0)),
                  pl.BlockSpec((K, N), lambda i: (0, 0)),
                  pl.BlockSpec((ts, N), lambda i: (i, 0))],
        out_specs=pl.BlockSpec((ts, N), lambda i: (i, 0)),
        out_shape=jax.ShapeDtypeStruct((T, N), F32),
        compiler_params=_cparams(("arbitrary",)),
        name="out_proj",
    )(a2, w.astype(BF16), h2)


ROUTE_TS = 256
RANK_SCALE = 2.0 ** 100
RANK_BIAS = 64.0


def _extract_top(s, n):
    rows = lax.broadcasted_iota(jnp.int32, (n, s.shape[1]), 0)
    vals = jnp.full((n, s.shape[1]), NEG, F32)
    work = s
    for j in range(n):
        m = jnp.max(work, axis=0, keepdims=True)
        vals = jnp.where(rows == j, m, vals)
        work = jnp.where(work >= m, -(RANK_BIAS + j) * RANK_SCALE, work)
    rank = jnp.where(work < -0.5 * RANK_BIAS * RANK_SCALE, work * (-1.0 / RANK_SCALE) - RANK_BIAS, float(n))
    return vals, rank


def _top_values(s, n):
    rows = lax.broadcasted_iota(jnp.int32, (n, s.shape[1]), 0)
    vals = jnp.full((n, s.shape[1]), NEG, F32)
    work = s
    for j in range(n):
        m = jnp.max(work, axis=0, keepdims=True)
        vals = jnp.where(rows == j, m, vals)
        if j + 1 < n:
            work = jnp.where(work >= m, NEG, work)
    return vals


def _candidate_sums(v1, v2):
    k = PEER_TOPK
    row8 = lax.broadcasted_iota(jnp.int32, (8, v1.shape[1]), 0)
    blocks = [v1[0:1] + v2]
    for a in range(1, 8):
        nb = k // (a + 1)
        blocks.append(jnp.where(row8 < nb, v1[a:a + 1] + v2[0:8], NEG))
    blocks.append(v1[8:16] + v2[0:1])
    return jnp.concatenate(blocks, axis=0)


def _route_kernel(h_ref, g_ref, wq_ref, sk1_ref, sk2_ref, xn_ref, c_ref, rk_ref, n_ref, r_ref):
    k = PEER_TOPK
    xn = _rms(h_ref[...], g_ref[...]).astype(BF16)
    xn_ref[...] = xn
    q = _dot(xn, wq_ref[...]).astype(BF16)
    for hd in range(PEER_HEADS):
        q1 = q[:, hd * 256:hd * 256 + 128]
        q2 = q[:, hd * 256 + 128:(hd + 1) * 256]
        s1 = _dot_nt(sk1_ref[...], q1)
        s2 = _dot_nt(sk2_ref[...], q2)
        v1, rank1 = _extract_top(s1, k)
        v2, rank2 = _extract_top(s2, k)
        tops = _top_values(_candidate_sums(v1, v2), k)
        z = jnp.sum(jnp.exp(tops - tops[0:1]), axis=0, keepdims=True)
        tau = tops[k - 1:k]
        cnt = jnp.zeros_like(v1)
        for b in range(k):
            cnt = cnt + jnp.where(v1 + v2[b:b + 1] >= tau, 1.0, 0.0)
        rank1_b = rank1.astype(BF16)
        nn = jnp.zeros(rank1_b.shape, BF16)
        for a in range(k):
            row = jnp.broadcast_to(cnt[a:a + 1].astype(BF16), nn.shape)
            nn = jnp.where(rank1_b == a, row, nn)
        n_ref[hd] = nn.astype(F32)
        r_ref[hd] = jnp.exp(s1 - v1[0:1]) / z
        rk_ref[hd] = rank2.astype(BF16)
        c_ref[hd] = jnp.exp(s2 - v2[0:1]).astype(BF16)


def _peer_route(h2, g, wq, sk1, sk2):
    T, D = h2.shape
    ts = ROUTE_TS
    nq = wq.shape[1]
    tab32 = jax.ShapeDtypeStruct((PEER_HEADS, PEER_KEYS, T), F32)
    tab16 = jax.ShapeDtypeStruct((PEER_HEADS, PEER_KEYS, T), BF16)
    tab_spec = pl.BlockSpec((PEER_HEADS, PEER_KEYS, ts), lambda i: (0, 0, i))
    return pl.pallas_call(
        _route_kernel,
        grid=(T // ts,),
        in_specs=[pl.BlockSpec((ts, D), lambda i: (i, 0)),
                  pl.BlockSpec((1, D), lambda i: (0, 0)),
                  pl.BlockSpec((D, nq), lambda i: (0, 0)),
                  pl.BlockSpec((PEER_KEYS, 128), lambda i: (0, 0)),
                  pl.BlockSpec((PEER_KEYS, 128), lambda i: (0, 0))],
        out_specs=[pl.BlockSpec((ts, D), lambda i: (i, 0)), tab_spec, tab_spec, tab_spec, tab_spec],
        out_shape=[jax.ShapeDtypeStruct((T, D), BF16), tab16, tab16, tab32, tab32],
        compiler_params=_cparams(("arbitrary",)),
        name="peer_route",
    )(h2, g.reshape(1, D).astype(F32), wq.astype(BF16), sk1.astype(BF16), sk2.astype(BF16))


PEER_TT = 512
PEER_TC = 256
PEER_ET = 2048
SQRT_HALF = math.sqrt(0.5)


def _fake_zero(block):
    bits = pltpu.bitcast(block, jnp.uint32)
    return pltpu.bitcast((bits >> 16) >> 16, BF16)


def _peer_dense_kernel(xn_ref, u_ref, vt_ref, c_ref, rk_ref, n_ref, r_ref, h_ref, o_ref, acc_ref):
    j = pl.program_id(1)
    tiles_per_step = PEER_ET // PEER_KEYS

    @pl.when(j == 0)
    def _():
        acc_ref[...] = jnp.zeros_like(acc_ref)

    z = _dot_nt(u_ref[...], xn_ref[...])
    pieces = [[] for _ in range(PEER_TT // PEER_TC)]
    w = jnp.zeros((16, PEER_TC), BF16)
    for ii in range(tiles_per_step):
        for tc in range(PEER_TT // PEER_TC):
            tok = slice(tc * PEER_TC, (tc + 1) * PEER_TC)
            nb, rb = [], []
            base8 = pl.multiple_of(j * tiles_per_step + (ii // 8) * 8, 8)
            for hd in range(PEER_HEADS):
                n8 = n_ref[hd, pl.ds(base8, 8), tok]
                r8 = r_ref[hd, pl.ds(base8, 8), tok]
                nb.append(jnp.broadcast_to(n8[ii % 8:ii % 8 + 1], (16, PEER_TC)).astype(BF16))
                rb.append(jnp.broadcast_to(r8[ii % 8:ii % 8 + 1], (16, PEER_TC)).astype(BF16))
            for kb in range(PEER_KEYS // 16):
                rows = slice(kb * 16, (kb + 1) * 16)
                erow = slice(ii * PEER_KEYS + kb * 16, ii * PEER_KEYS + (kb + 1) * 16)
                w = _fake_zero(w)
                for hd in range(PEER_HEADS):
                    w = w + jnp.where(rk_ref[hd, rows, tok] < nb[hd], rb[hd] * c_ref[hd, rows, tok],
                                      jnp.zeros_like(w))
                zz = z[erow, tok]
                act = 0.5 * zz * (1.0 + lax.erf(zz * SQRT_HALF))
                pieces[tc].append(w * act.astype(BF16))
    p = jnp.concatenate([jnp.concatenate(col, axis=0) for col in pieces], axis=1)
    acc_ref[...] += _dot(vt_ref[...], p)

    @pl.when(j == pl.num_programs(1) - 1)
    def _():
        o_ref[...] = h_ref[...] + acc_ref[...].T


def _peer_dense(xn, u_bf, vt_bf, c, rk, n, r, h2):
    T, D = h2.shape
    tt, et = PEER_TT, PEER_ET
    tab_spec = pl.BlockSpec((PEER_HEADS, PEER_KEYS, tt), lambda i, j: (0, 0, i))
    return pl.pallas_call(
        _peer_dense_kernel,
        grid=(T // tt, PEER_EXPERTS // et),
        in_specs=[pl.BlockSpec((tt, D), lambda i, j: (i, 0)),
                  pl.BlockSpec((et, D), lambda i, j: (j, 0)),
                  pl.BlockSpec((D, et), lambda i, j: (0, j)),
                  tab_spec, tab_spec, tab_spec, tab_spec,
                  pl.BlockSpec((tt, D), lambda i, j: (i, 0))],
        out_specs=pl.BlockSpec((tt, D), lambda i, j: (i, 0)),
        out_shape=jax.ShapeDtypeStruct((T, D), F32),
        scratch_shapes=[pltpu.VMEM((D, tt), F32)],
        compiler_params=_cparams(("arbitrary", "arbitrary")),
        name="peer_dense",
    )(xn, u_bf, vt_bf, c, rk, n, r, h2)


def _peer_layer(h2, g, wq, sk1, sk2, u_tab, v_tab):
    xn, c, rk, n, r = _peer_route(h2, g, wq, sk1, sk2)
    return _peer_dense(xn, u_tab.astype(BF16), v_tab.astype(BF16).T, c, rk, n, r, h2)


def _final_norm_kernel(x_ref, g_ref, o_ref):
    o_ref[...] = _rms(x_ref[...], g_ref[...])


def _final_norm(h2, g):
    T, D = h2.shape
    ts = PROJ_TS
    return pl.pallas_call(
        _final_norm_kernel,
        grid=(T // ts,),
        in_specs=[pl.BlockSpec((ts, D), lambda i: (i, 0)), pl.BlockSpec((1, D), lambda i: (0, 0))],
        out_specs=pl.BlockSpec((ts, D), lambda i: (i, 0)),
        out_shape=jax.ShapeDtypeStruct((T, D), F32),
        compiler_params=_cparams(("arbitrary",)),
        name="final_norm",
    )(h2, g.reshape(1, D).astype(F32))


def kernel(x, meta_tokens, a_norm_g, a_pw1_w, a_pw1_b, a_dw_w, a_dw_b, a_ln_g, a_ln_b, a_pw2_w, a_pw2_b, kv_norm_g, w_kv, b_norm_g, b_wq, b_lambda_q1, b_lambda_k1, b_lambda_q2, b_lambda_k2, b_subln_g, b_wo, f_norm_g, f_wq, f_subkey1, f_subkey2, f_u, f_v, final_norm_g):
    B, S, D = x.shape
    L = N_META + S
    Lp = -(-L // BLOCK) * BLOCK
    T = B * Lp
    meta = jnp.broadcast_to(meta_tokens.astype(x.dtype)[None], (B, N_META, D))
    h = jnp.concatenate([meta, x, jnp.zeros((B, Lp - L, D), x.dtype)], axis=1)
    cos_t, sin_t = _rope_tables(Lp)
    nk = 2 * N_HEADS * HEAD_DIM
    k3 = v3 = None
    for layer in range(DEPTH):
        if layer < N_A_LAYERS:
            i = layer
            h = _conformer_layer(h, a_norm_g[i], a_pw1_w[i], a_pw1_b[i], a_dw_w[i], a_dw_b[i],
                                 a_ln_g[i], a_ln_b[i], a_pw2_w[i], a_pw2_b[i])
        else:
            j = layer - N_A_LAYERS
            lam_init = 0.8 - 0.6 * math.exp(-0.3 * layer)
            h2 = h.reshape(T, D)
            q = _norm_proj(h2, b_norm_g[j], b_wq[j], cos_t, sin_t, rope=True,
                           scale=HEAD_DIM ** -0.5, lp=Lp)
            lam_rows = jnp.zeros((8, 128), F32).at[0:4, 0:HEAD_DIM].set(
                jnp.stack([b_lambda_q1[j], b_lambda_k1[j], b_lambda_q2[j], b_lambda_k2[j]]).astype(F32))
            o = _diff_attention(q.reshape(B, Lp, nk), k3, v3, lam_rows, b_subln_g[j], lam_init)
            h = _matmul_res(o.reshape(T, N_HEADS * V_DIM), b_wo[j], h2).reshape(B, Lp, D)
        h2 = _peer_layer(h.reshape(T, D), f_norm_g[layer], f_wq[layer], f_subkey1[layer],
                         f_subkey2[layer], f_u[layer], f_v[layer])
        h = h2.reshape(B, Lp, D)
        if layer == N_A_LAYERS - 1:
            k3 = _norm_proj(h2, kv_norm_g, w_kv[:, :nk], cos_t, sin_t, rope=True, scale=1.0,
                            lp=Lp).reshape(B, Lp, nk)
            v3 = _norm_proj(h2, kv_norm_g, w_kv[:, nk:], cos_t, sin_t, rope=False, scale=1.0,
                            lp=Lp).reshape(B, Lp, N_HEADS * V_DIM)
    out = _final_norm(h.reshape(T, D), final_norm_g).reshape(B, Lp, D)
    return out[:, N_META:N_META + S]
```

```python
import functools
import math

import jax
import jax.numpy as jnp
from jax import lax
from jax.experimental import pallas as pl
from jax.experimental.pallas import tpu as pltpu

D_MODEL = 1024
N_META = 16
BLOCK = 128
DEPTH = 4
N_A_LAYERS = DEPTH // 2
CONV_WIDTH = 31
N_HEADS = 8
HEAD_DIM = 64
V_DIM = 128
ROT_DIM = 16
ROPE_THETA = 500000.0
PEER_HEADS = 8
PEER_KEYS = 128
PEER_EXPERTS = PEER_KEYS * PEER_KEYS
PEER_TOPK = 16
EPS = 1e-6

NEG = -1e30
VMEM_LIMIT = 52 * 1024 * 1024

F32 = jnp.float32
BF16 = jnp.bfloat16


def _cparams(sem, flags=None):
    return pltpu.CompilerParams(dimension_semantics=sem, vmem_limit_bytes=VMEM_LIMIT, flags=flags)


def _rms(x, g):
    return x * lax.rsqrt(jnp.mean(x * x, axis=-1, keepdims=True) + EPS) * g


def _dot(a, b):
    return jnp.dot(a, b, preferred_element_type=F32)


def _dot_nt(a, b):
    return lax.dot_general(a, b, (((1,), (1,)), ((), ())), preferred_element_type=F32)


CONF_TS = 384
CONF_HALO = 32
CONF_RB = 32
CONF_CB = 256


def _conformer_kernel(x_ref, ng_ref, w1a_ref, w1g_ref, b1a_ref, b1g_ref, dww_ref, dwb_ref,
                      lng_ref, lnb_ref, w2_ref, b2_ref, o_ref, buf_ref, cbuf_ref):
    ts = CONF_TS

    @pl.when(pl.program_id(1) == 0)
    def _():
        buf_ref[0:CONF_HALO, :] = jnp.zeros((CONF_HALO, D_MODEL), F32)

    x = x_ref[0]
    n = _rms(x, ng_ref[...]).astype(BF16)
    a = _dot(n, w1a_ref[...]) + b1a_ref[...]
    g = _dot(n, w1g_ref[...]) + b1g_ref[...]
    buf_ref[CONF_HALO:CONF_HALO + ts, :] = a * jax.nn.sigmoid(g)

    off0 = CONF_HALO - (CONV_WIDTH - 1)

    def row_body(rb, carry):
        r0 = pl.multiple_of(rb * CONF_RB, CONF_RB)
        for cb in range(D_MODEL // CONF_CB):
            cols = slice(cb * CONF_CB, (cb + 1) * CONF_CB)
            acc = jnp.broadcast_to(dwb_ref[:, cols], (CONF_RB, CONF_CB))
            win = buf_ref[pl.ds(r0, CONF_RB + CONF_HALO), cols]
            shifted = [win] + [pltpu.roll(win, CONF_RB + CONF_HALO - j, axis=0) for j in range(1, 8)]
            for k in range(CONV_WIDTH):
                q8, j = divmod(off0 + k, 8)
                acc = acc + dww_ref[k:k + 1, cols] * shifted[j][8 * q8:8 * q8 + CONF_RB]
            cbuf_ref[pl.ds(r0, CONF_RB), cols] = acc
        return carry

    lax.fori_loop(0, ts // CONF_RB, row_body, 0)
    buf_ref[0:CONF_HALO, :] = buf_ref[ts:ts + CONF_HALO, :]

    c = cbuf_ref[...]
    mu = jnp.mean(c, axis=-1, keepdims=True)
    cc = c - mu
    y = cc * lax.rsqrt(jnp.mean(cc * cc, axis=-1, keepdims=True) + EPS)
    y = y * lng_ref[...] + lnb_ref[...]
    y = (y * jax.nn.sigmoid(y)).astype(BF16)
    o_ref[0] = x + _dot(y, w2_ref[...]) + b2_ref[...]


def _conformer_layer(h3, ng, w1, b1, dww, dwb, lng, lnb, w2, b2):
    B, Lp, D = h3.shape
    ts = CONF_TS
    row = lambda v: v.reshape(1, -1).astype(F32)
    const = lambda shape: pl.BlockSpec(shape, lambda b, t: (0,) * len(shape))
    return pl.pallas_call(
        _conformer_kernel,
        grid=(B, Lp // ts),
        in_specs=[
            pl.BlockSpec((1, ts, D), lambda b, t: (b, t, 0)),
            const((1, D)), const((D, D)), const((D, D)), const((1, D)), const((1, D)),
            const((CONV_WIDTH, D)), const((1, D)), const((1, D)), const((1, D)),
            const((D, D)), const((1, D)),
        ],
        out_specs=pl.BlockSpec((1, ts, D), lambda b, t: (b, t, 0)),
        out_shape=jax.ShapeDtypeStruct((B, Lp, D), F32),
        scratch_shapes=[pltpu.VMEM((ts + CONF_HALO, D), F32), pltpu.VMEM((ts, D), F32)],
        compiler_params=_cparams(("arbitrary", "arbitrary")),
        name="conformer",
    )(h3, row(ng), w1[:, :D].astype(BF16), w1[:, D:].astype(BF16), row(b1[:D]), row(b1[D:]),
      dww.astype(F32), row(dwb), row(lng), row(lnb), w2.astype(BF16), row(b2))


PROJ_TS = 384


def _proj_kernel(x_ref, g_ref, w_ref, cos_ref, sin_ref, o_ref, *, rope, scale):
    n = _rms(x_ref[...], g_ref[...]).astype(BF16)
    y = _dot(n, w_ref[...])
    if rope:
        cos = cos_ref[...]
        sin = sin_ref[...]
        lane = lax.broadcasted_iota(jnp.int32, cos.shape, 1) % HEAD_DIM
        for cb in range(y.shape[1] // 128):
            yb = y[:, cb * 128:(cb + 1) * 128]
            fwd = pltpu.roll(yb, 128 - ROT_DIM // 2, axis=1)
            bwd = pltpu.roll(yb, ROT_DIM // 2, axis=1)
            partner = jnp.where(lane < ROT_DIM // 2, fwd, bwd)
            o_ref[:, cb * 128:(cb + 1) * 128] = ((yb * cos + partner * sin) * scale).astype(BF16)
    else:
        o_ref[...] = (y * scale).astype(BF16)


def _norm_proj(h2, g, w, cos_t, sin_t, *, rope, scale, lp):
    T, D = h2.shape
    N = w.shape[1]
    ts = PROJ_TS
    nt = lp // ts
    return pl.pallas_call(
        functools.partial(_proj_kernel, rope=rope, scale=scale),
        grid=(T // ts,),
        in_specs=[
            pl.BlockSpec((ts, D), lambda i: (i, 0)),
            pl.BlockSpec((1, D), lambda i: (0, 0)),
            pl.BlockSpec((D, N), lambda i: (0, 0)),
            pl.BlockSpec((ts, 128), lambda i: (i % nt, 0)),
            pl.BlockSpec((ts, 128), lambda i: (i % nt, 0)),
        ],
        out_specs=pl.BlockSpec((ts, N), lambda i: (i, 0)),
        out_shape=jax.ShapeDtypeStruct((T, N), BF16),
        compiler_params=_cparams(("arbitrary",)),
        name="norm_proj",
    )(h2, g.reshape(1, D).astype(F32), w.astype(BF16), cos_t, sin_t)


def _rope_tables(lp):
    inv = ROPE_THETA ** (-jnp.arange(0, ROT_DIM, 2, dtype=F32) / ROT_DIM)
    d = jnp.arange(128) % HEAD_DIM
    ang = jnp.arange(lp, dtype=F32)[:, None] * inv[d % (ROT_DIM // 2)][None, :]
    cos_t = jnp.where(d[None, :] < ROT_DIM, jnp.cos(ang), 1.0)
    sin_t = jnp.where(d[None, :] < ROT_DIM // 2, -jnp.sin(ang),
                      jnp.where(d[None, :] < ROT_DIM, jnp.sin(ang), 0.0))
    return cos_t.astype(F32), sin_t.astype(F32)


ATT_T = 384
ATT_G = 4


def _attn_t_kernel(qt_ref, k_ref, vt_ref, lam_ref, sgt_ref, o_ref, *, lam_init):
    t = ATT_T
    qi = pl.program_id(2)
    qqs = []
    for g in range(ATT_G):
        qt = qt_ref[0, g, 0]
        row = lax.broadcasted_iota(jnp.int32, qt.shape, 0)
        zero = jnp.zeros_like(qt)
        qqs.append(jnp.concatenate([jnp.where(row < HEAD_DIM, qt, zero),
                                    jnp.where(row >= HEAD_DIM, qt, zero)], axis=1))

    def step(kc, carry, masked):
        k0 = pl.multiple_of(kc * t, t)
        kk = k_ref[0, pl.ds(k0, t), :]
        ss = [_dot(kk[:, g * 128:(g + 1) * 128], qqs[g]) for g in range(ATT_G)]
        if masked:
            r = lax.broadcasted_iota(jnp.int32, (t, 128), 0)
            c = lax.broadcasted_iota(jnp.int32, (t, 128), 1)
        out = []
        for g in range(ATT_G):
            m, l, acc = carry[g]
            ms, ls, ps = [], [], []
            for lb in range(2 * t // 128):
                sl = ss[g][:, lb * 128:(lb + 1) * 128]
                if masked:
                    sl = jnp.where(r <= c + (lb * 128) % t, sl, NEG)
                ml = m[:, lb * 128:(lb + 1) * 128]
                m_new = jnp.maximum(ml, jnp.max(sl, axis=0, keepdims=True))
                ms.append(m_new)
                ls.append(jnp.exp(ml - m_new))
                ps.append(jnp.exp(sl - m_new))
            m_new = jnp.concatenate(ms, axis=1)
            alpha = jnp.concatenate(ls, axis=1)
            p = jnp.concatenate(ps, axis=1)
            l = alpha * l + jnp.sum(p, axis=0, keepdims=True)
            acc = alpha * acc + _dot(vt_ref[0, g, kc], p.astype(BF16))
            out.append((m_new, l, acc))
        return tuple(out)

    init = tuple((jnp.full((1, 2 * t), NEG, F32), jnp.zeros((1, 2 * t), F32),
                  jnp.zeros((V_DIM, 2 * t), F32)) for _ in range(ATT_G))
    carry = lax.fori_loop(0, qi, lambda kc, c: step(kc, c, False), init)
    carry = step(qi, carry, True)
    lp = lam_ref[...]
    lam = (jnp.exp(jnp.sum(lp[0:1] * lp[1:2], axis=-1, keepdims=True))
           - jnp.exp(jnp.sum(lp[2:3] * lp[3:4], axis=-1, keepdims=True)) + lam_init)
    sgt = jnp.concatenate([sgt_ref[...]] * (t // 128), axis=1)
    for g in range(ATT_G):
        m, l, acc = carry[g]
        o = acc / l
        od = o[:, :t] - lam * o[:, t:]
        od = od * lax.rsqrt(jnp.mean(od * od, axis=0, keepdims=True) + EPS)
        od = od * sgt * (1.0 - lam_init)
        o_ref[0, :, g * 128:(g + 1) * 128] = od.T.astype(BF16)


def _diff_attention_t(q3, k3, v3, lam_rows, subln_g, lam_init):
    B, Lp, _ = q3.shape
    t, g = ATT_T, ATT_G
    nt = Lp // t
    to_t = lambda a: a.reshape(B, nt, t, N_HEADS, 128).transpose(0, 3, 1, 4, 2)
    sgt = jnp.broadcast_to(subln_g.astype(F32)[:, None], (V_DIM, 128))
    return pl.pallas_call(
        functools.partial(_attn_t_kernel, lam_init=lam_init),
        grid=(B, N_HEADS // g, nt),
        in_specs=[
            pl.BlockSpec((1, g, 1, 128, t), lambda b, h, i: (b, h, i, 0, 0)),
            pl.BlockSpec((1, Lp, 128 * g), lambda b, h, i: (b, 0, h)),
            pl.BlockSpec((1, g, nt, V_DIM, t), lambda b, h, i: (b, h, 0, 0, 0)),
            pl.BlockSpec((8, 128), lambda b, h, i: (0, 0)),
            pl.BlockSpec((V_DIM, 128), lambda b, h, i: (0, 0)),
        ],
        out_specs=pl.BlockSpec((1, t, 128 * g), lambda b, h, i: (b, i, h)),
        out_shape=jax.ShapeDtypeStruct((B, Lp, N_HEADS * V_DIM), BF16),
        compiler_params=_cparams(("arbitrary", "arbitrary", "arbitrary")),
        name="diff_attn",
    )(to_t(q3), k3, to_t(v3), lam_rows, sgt)


def _matmul_res_kernel(a_ref, w_ref, h_ref, o_ref):
    o_ref[...] = h_ref[...] + _dot(a_ref[...], w_ref[...])


def _matmul_res(a2, w, h2):
    T, K = a2.shape
    N = w.shape[1]
    ts = PROJ_TS
    return pl.pallas_call(
        _matmul_res_kernel,
        grid=(T // ts,),
        in_specs=[pl.BlockSpec((ts, K), lambda i: (i, 0)),
                  pl.BlockSpec((K, N), lambda i: (0, 0)),
                  pl.BlockSpec((ts, N), lambda i: (i, 0))],
        out_specs=pl.BlockSpec((ts, N), lambda i: (i, 0)),
        out_shape=jax.ShapeDtypeStruct((T, N), F32),
        compiler_params=_cparams(("arbitrary",)),
        name="out_proj",
    )(a2, w.astype(BF16), h2)


ROUTE_TS = 256
RANK_SCALE = 2.0 ** 100
RANK_BIAS = 64.0


def _extract_top(s, n):
    rows = lax.broadcasted_iota(jnp.int32, (n, s.shape[1]), 0)
    vals = jnp.full((n, s.shape[1]), NEG, F32)
    work = s
    for j in range(n):
        m = jnp.max(work, axis=0, keepdims=True)
        vals = jnp.where(rows == j, m, vals)
        work = jnp.where(work >= m, -(RANK_BIAS + j) * RANK_SCALE, work)
    rank = jnp.where(work < -0.5 * RANK_BIAS * RANK_SCALE, work * (-1.0 / RANK_SCALE) - RANK_BIAS, float(n))
    return vals, rank


def _top_values(s, n):
    rows = lax.broadcasted_iota(jnp.int32, (n, s.shape[1]), 0)
    vals = jnp.full((n, s.shape[1]), NEG, F32)
    work = s
    for j in range(n):
        m = jnp.max(work, axis=0, keepdims=True)
        vals = jnp.where(rows == j, m, vals)
        if j + 1 < n:
            work = jnp.where(work >= m, NEG, work)
    return vals


def _candidate_sums(v1, v2):
    k = PEER_TOPK
    row8 = lax.broadcasted_iota(jnp.int32, (8, v1.shape[1]), 0)
    blocks = [v1[0:1] + v2]
    for a in range(1, 8):
        nb = k // (a + 1)
        blocks.append(jnp.where(row8 < nb, v1[a:a + 1] + v2[0:8], NEG))
    blocks.append(v1[8:16] + v2[0:1])
    return jnp.concatenate(blocks, axis=0)


def _route_kernel(h_ref, g_ref, wq_ref, sk1_ref, sk2_ref, xn_ref, c_ref, rk_ref, n_ref, r_ref):
    k = PEER_TOPK
    xn = _rms(h_ref[...], g_ref[...]).astype(BF16)
    xn_ref[...] = xn
    q = _dot(xn, wq_ref[...]).astype(BF16)
    for hd in range(PEER_HEADS):
        q1 = q[:, hd * 256:hd * 256 + 128]
        q2 = q[:, hd * 256 + 128:(hd + 1) * 256]
        s1 = _dot_nt(sk1_ref[...], q1)
        s2 = _dot_nt(sk2_ref[...], q2)
        v1, rank1 = _extract_top(s1, k)
        v2, rank2 = _extract_top(s2, k)
        tops = _top_values(_candidate_sums(v1, v2), k)
        z = jnp.sum(jnp.exp(tops - tops[0:1]), axis=0, keepdims=True)
        tau = tops[k - 1:k]
        cnt = jnp.zeros_like(v1)
        for b in range(k):
            cnt = cnt + jnp.where(v1 + v2[b:b + 1] >= tau, 1.0, 0.0)
        rank1_b = rank1.astype(BF16)
        nn = jnp.zeros(rank1_b.shape, BF16)
        for a in range(k):
            row = jnp.broadcast_to(cnt[a:a + 1].astype(BF16), nn.shape)
            nn = jnp.where(rank1_b == a, row, nn)
        n_ref[hd] = nn.astype(F32)
        r_ref[hd] = jnp.exp(s1 - v1[0:1]) / z
        rk_ref[hd] = rank2.astype(BF16)
        c_ref[hd] = jnp.exp(s2 - v2[0:1]).astype(BF16)


def _peer_route(h2, g, wq, sk1, sk2):
    T, D = h2.shape
    ts = ROUTE_TS
    nq = wq.shape[1]
    tab32 = jax.ShapeDtypeStruct((PEER_HEADS, PEER_KEYS, T), F32)
    tab16 = jax.ShapeDtypeStruct((PEER_HEADS, PEER_KEYS, T), BF16)
    tab_spec = pl.BlockSpec((PEER_HEADS, PEER_KEYS, ts), lambda i: (0, 0, i))
    return pl.pallas_call(
        _route_kernel,
        grid=(T // ts,),
        in_specs=[pl.BlockSpec((ts, D), lambda i: (i, 0)),
                  pl.BlockSpec((1, D), lambda i: (0, 0)),
                  pl.BlockSpec((D, nq), lambda i: (0, 0)),
                  pl.BlockSpec((PEER_KEYS, 128), lambda i: (0, 0)),
                  pl.BlockSpec((PEER_KEYS, 128), lambda i: (0, 0))],
        out_specs=[pl.BlockSpec((ts, D), lambda i: (i, 0)), tab_spec, tab_spec, tab_spec, tab_spec],
        out_shape=[jax.ShapeDtypeStruct((T, D), BF16), tab16, tab16, tab32, tab32],
        compiler_params=_cparams(("arbitrary",)),
        name="peer_route",
    )(h2, g.reshape(1, D).astype(F32), wq.astype(BF16), sk1.astype(BF16), sk2.astype(BF16))


PEER_TT = 512
PEER_TC = 256
PEER_ET = 2048
SQRT_HALF = math.sqrt(0.5)


def _fake_zero(block):
    bits = pltpu.bitcast(block, jnp.uint32)
    return pltpu.bitcast((bits >> 16) >> 16, BF16)


def _peer_dense_kernel(xn_ref, u_ref, vt_ref, c_ref, rk_ref, n_ref, r_ref, h_ref, o_ref, acc_ref):
    j = pl.program_id(1)
    tiles_per_step = PEER_ET // PEER_KEYS

    @pl.when(j == 0)
    def _():
        acc_ref[...] = jnp.zeros_like(acc_ref)

    z = _dot_nt(u_ref[...], xn_ref[...])
    pieces = [[] for _ in range(PEER_TT // PEER_TC)]
    w = jnp.zeros((16, PEER_TC), BF16)
    for ii in range(tiles_per_step):
        for tc in range(PEER_TT // PEER_TC):
            tok = slice(tc * PEER_TC, (tc + 1) * PEER_TC)
            nb, rb = [], []
            base8 = pl.multiple_of(j * tiles_per_step + (ii // 8) * 8, 8)
            for hd in range(PEER_HEADS):
                n8 = n_ref[hd, pl.ds(base8, 8), tok]
                r8 = r_ref[hd, pl.ds(base8, 8), tok]
                nb.append(jnp.broadcast_to(n8[ii % 8:ii % 8 + 1], (16, PEER_TC)).astype(BF16))
                rb.append(jnp.broadcast_to(r8[ii % 8:ii % 8 + 1], (16, PEER_TC)).astype(BF16))
            for kb in range(PEER_KEYS // 16):
                rows = slice(kb * 16, (kb + 1) * 16)
                erow = slice(ii * PEER_KEYS + kb * 16, ii * PEER_KEYS + (kb + 1) * 16)
                w = _fake_zero(w)
                for hd in range(PEER_HEADS):
                    w = w + jnp.where(rk_ref[hd, rows, tok] < nb[hd], rb[hd] * c_ref[hd, rows, tok],
                                      jnp.zeros_like(w))
                zz = z[erow, tok]
                act = 0.5 * zz * (1.0 + lax.erf(zz * SQRT_HALF))
                pieces[tc].append(w * act.astype(BF16))
    p = jnp.concatenate([jnp.concatenate(col, axis=0) for col in pieces], axis=1)
    acc_ref[...] += _dot(vt_ref[...], p)

    @pl.when(j == pl.num_programs(1) - 1)
    def _():
        o_ref[...] = h_ref[...] + acc_ref[...].T


def _peer_dense(xn, u_bf, vt_bf, c, rk, n, r, h2):
    T, D = h2.shape
    tt, et = PEER_TT, PEER_ET
    tab_spec = pl.BlockSpec((PEER_HEADS, PEER_KEYS, tt), lambda i, j: (0, 0, i))
    return pl.pallas_call(
        _peer_dense_kernel,
        grid=(T // tt, PEER_EXPERTS // et),
        in_specs=[pl.BlockSpec((tt, D), lambda i, j: (i, 0)),
                  pl.BlockSpec((et, D), lambda i, j: (j, 0)),
                  pl.BlockSpec((D, et), lambda i, j: (0, j)),
                  tab_spec, tab_spec, tab_spec, tab_spec,
                  pl.BlockSpec((tt, D), lambda i, j: (i, 0))],
        out_specs=pl.BlockSpec((tt, D), lambda i, j: (i, 0)),
        out_shape=jax.ShapeDtypeStruct((T, D), F32),
        scratch_shapes=[pltpu.VMEM((D, tt), F32)],
        compiler_params=_cparams(("arbitrary", "arbitrary")),
        name="peer_dense",
    )(xn, u_bf, vt_bf, c, rk, n, r, h2)


def _peer_layer(h2, g, wq, sk1, sk2, u_tab, v_tab):
    xn, c, rk, n, r = _peer_route(h2, g, wq, sk1, sk2)
    return _peer_dense(xn, u_tab.astype(BF16), v_tab.astype(BF16).T, c, rk, n, r, h2)


def _final_norm_kernel(x_ref, g_ref, o_ref):
    o_ref[...] = _rms(x_ref[...], g_ref[...])


def _final_norm(h2, g):
    T, D = h2.shape
    ts = PROJ_TS
    return pl.pallas_call(
        _final_norm_kernel,
        grid=(T // ts,),
        in_specs=[pl.BlockSpec((ts, D), lambda i: (i, 0)), pl.BlockSpec((1, D), lambda i: (0, 0))],
        out_specs=pl.BlockSpec((ts, D), lambda i: (i, 0)),
        out_shape=jax.ShapeDtypeStruct((T, D), F32),
        compiler_params=_cparams(("arbitrary",)),
        name="final_norm",
    )(h2, g.reshape(1, D).astype(F32))


def kernel(x, meta_tokens, a_norm_g, a_pw1_w, a_pw1_b, a_dw_w, a_dw_b, a_ln_g, a_ln_b, a_pw2_w, a_pw2_b, kv_norm_g, w_kv, b_norm_g, b_wq, b_lambda_q1, b_lambda_k1, b_lambda_q2, b_lambda_k2, b_subln_g, b_wo, f_norm_g, f_wq, f_subkey1, f_subkey2, f_u, f_v, final_norm_g):
    B, S, D = x.shape
    L = N_META + S
    Lp = -(-L // BLOCK) * BLOCK
    T = B * Lp
    meta = jnp.broadcast_to(meta_tokens.astype(x.dtype)[None], (B, N_META, D))
    h = jnp.concatenate([meta, x, jnp.zeros((B, Lp - L, D), x.dtype)], axis=1)
    cos_t, sin_t = _rope_tables(Lp)
    nk = 2 * N_HEADS * HEAD_DIM
    k3 = v3 = None
    for layer in range(DEPTH):
        if layer < N_A_LAYERS:
            i = layer
            h = _conformer_layer(h, a_norm_g[i], a_pw1_w[i], a_pw1_b[i], a_dw_w[i], a_dw_b[i],
                                 a_ln_g[i], a_ln_b[i], a_pw2_w[i], a_pw2_b[i])
        else:
            j = layer - N_A_LAYERS
            lam_init = 0.8 - 0.6 * math.exp(-0.3 * layer)
            h2 = h.reshape(T, D)
            q = _norm_proj(h2, b_norm_g[j], b_wq[j], cos_t, sin_t, rope=True,
                           scale=HEAD_DIM ** -0.5, lp=Lp)
            lam_rows = jnp.zeros((8, 128), F32).at[0:4, 0:HEAD_DIM].set(
                jnp.stack([b_lambda_q1[j], b_lambda_k1[j], b_lambda_q2[j], b_lambda_k2[j]]).astype(F32))
            o = _diff_attention_t(q.reshape(B, Lp, nk), k3, v3, lam_rows, b_subln_g[j], lam_init)
            h = _matmul_res(o.reshape(T, N_HEADS * V_DIM), b_wo[j], h2).reshape(B, Lp, D)
        h2 = _peer_layer(h.reshape(T, D), f_norm_g[layer], f_wq[layer], f_subkey1[layer],
                         f_subkey2[layer], f_u[layer], f_v[layer])
        h = h2.reshape(B, Lp, D)
        if layer == N_A_LAYERS - 1:
            k3 = _norm_proj(h2, kv_norm_g, w_kv[:, :nk], cos_t, sin_t, rope=True, scale=1.0,
                            lp=Lp).reshape(B, Lp, nk)
            v3 = _norm_proj(h2, kv_norm_g, w_kv[:, nk:], cos_t, sin_t, rope=False, scale=1.0,
                            lp=Lp).reshape(B, Lp, N_HEADS * V_DIM)
    out = _final_norm(h.reshape(T, D), final_norm_g).reshape(B, Lp, D)
    return out[:, N_META:N_META + S]
```

```python
import functools
import math

import jax
import jax.numpy as jnp
from jax import lax
from jax.experimental import pallas as pl
from jax.experimental.pallas import tpu as pltpu

D_MODEL = 1024
N_META = 16
BLOCK = 128
DEPTH = 4
N_A_LAYERS = DEPTH // 2
CONV_WIDTH = 31
N_HEADS = 8
HEAD_DIM = 64
V_DIM = 128
ROT_DIM = 16
ROPE_THETA = 500000.0
PEER_HEADS = 8
PEER_KEYS = 128
PEER_EXPERTS = PEER_KEYS * PEER_KEYS
PEER_TOPK = 16
EPS = 1e-6

NEG = -1e30
VMEM_LIMIT = 52 * 1024 * 1024

F32 = jnp.float32
BF16 = jnp.bfloat16


def _cparams(sem, flags=None):
    return pltpu.CompilerParams(dimension_semantics=sem, vmem_limit_bytes=VMEM_LIMIT, flags=flags)


def _rms(x, g):
    return x * lax.rsqrt(jnp.mean(x * x, axis=-1, keepdims=True) + EPS) * g


def _dot(a, b):
    return jnp.dot(a, b, preferred_element_type=F32)


def _dot_nt(a, b):
    return lax.dot_general(a, b, (((1,), (1,)), ((), ())), preferred_element_type=F32)


CONF_TS = 384
CONF_HALO = 32
CONF_RB = 32
CONF_CB = 256


def _conformer_kernel(x_ref, *rest):
    _conformer_body(x_ref[0], *rest)


def _conformer_first_kernel(xp_ref, xc_ref, meta_ref, *rest, seq):
    ts = CONF_TS
    t = pl.program_id(1)
    x = jnp.concatenate([xp_ref[0, ts - N_META:ts, :], xc_ref[0, 0:ts - N_META, :]], axis=0)
    grow = t * ts + lax.broadcasted_iota(jnp.int32, (ts, 1), 0)
    meta = jnp.concatenate([meta_ref[...], jnp.zeros((ts - N_META, D_MODEL), F32)], axis=0)
    x = jnp.where(grow < N_META, meta, jnp.where(grow < N_META + seq, x, 0.0))
    _conformer_body(x, *rest)


def _conformer_body(x, ng_ref, w1a_ref, w1g_ref, b1a_ref, b1g_ref, dww_ref, dwb_ref,
                    lng_ref, lnb_ref, w2_ref, b2_ref, o_ref, buf_ref, cbuf_ref):
    ts = CONF_TS

    @pl.when(pl.program_id(1) == 0)
    def _():
        buf_ref[0:CONF_HALO, :] = jnp.zeros((CONF_HALO, D_MODEL), F32)

    n = _rms(x, ng_ref[...]).astype(BF16)
    a = _dot(n, w1a_ref[...]) + b1a_ref[...]
    g = _dot(n, w1g_ref[...]) + b1g_ref[...]
    buf_ref[CONF_HALO:CONF_HALO + ts, :] = a * jax.nn.sigmoid(g)

    off0 = CONF_HALO - (CONV_WIDTH - 1)

    def row_body(rb, carry):
        r0 = pl.multiple_of(rb * CONF_RB, CONF_RB)
        for cb in range(D_MODEL // CONF_CB):
            cols = slice(cb * CONF_CB, (cb + 1) * CONF_CB)
            acc = jnp.broadcast_to(dwb_ref[:, cols], (CONF_RB, CONF_CB))
            win = buf_ref[pl.ds(r0, CONF_RB + CONF_HALO), cols]
            shifted = [win] + [pltpu.roll(win, CONF_RB + CONF_HALO - j, axis=0) for j in range(1, 8)]
            for k in range(CONV_WIDTH):
                q8, j = divmod(off0 + k, 8)
                acc = acc + dww_ref[k:k + 1, cols] * shifted[j][8 * q8:8 * q8 + CONF_RB]
            cbuf_ref[pl.ds(r0, CONF_RB), cols] = acc
        return carry

    lax.fori_loop(0, ts // CONF_RB, row_body, 0)
    buf_ref[0:CONF_HALO, :] = buf_ref[ts:ts + CONF_HALO, :]

    c = cbuf_ref[...]
    mu = jnp.mean(c, axis=-1, keepdims=True)
    cc = c - mu
    y = cc * lax.rsqrt(jnp.mean(cc * cc, axis=-1, keepdims=True) + EPS)
    y = y * lng_ref[...] + lnb_ref[...]
    y = (y * jax.nn.sigmoid(y)).astype(BF16)
    o_ref[0] = x + _dot(y, w2_ref[...]) + b2_ref[...]


def _conformer_layer(h3, ng, w1, b1, dww, dwb, lng, lnb, w2, b2, *, meta=None, lp=None):
    B, _, D = h3.shape
    Lp = h3.shape[1] if meta is None else lp
    ts = CONF_TS
    row = lambda v: v.reshape(1, -1).astype(F32)
    const = lambda shape: pl.BlockSpec(shape, lambda b, t: (0,) * len(shape))
    if meta is None:
        body, lead_specs, lead = _conformer_kernel, [pl.BlockSpec((1, ts, D), lambda b, t: (b, t, 0))], (h3,)
    else:
        last = (h3.shape[1] - 1) // ts
        body = functools.partial(_conformer_first_kernel, seq=h3.shape[1])
        lead_specs = [pl.BlockSpec((1, ts, D), lambda b, t: (b, jnp.maximum(t - 1, 0), 0)),
                      pl.BlockSpec((1, ts, D), lambda b, t: (b, jnp.minimum(t, last), 0)),
                      const((N_META, D))]
        lead = (h3, h3, meta.astype(F32))
    return pl.pallas_call(
        body,
        grid=(B, Lp // ts),
        in_specs=lead_specs + [
            const((1, D)), const((D, D)), const((D, D)), const((1, D)), const((1, D)),
            const((CONV_WIDTH, D)), const((1, D)), const((1, D)), const((1, D)),
            const((D, D)), const((1, D)),
        ],
        out_specs=pl.BlockSpec((1, ts, D), lambda b, t: (b, t, 0)),
        out_shape=jax.ShapeDtypeStruct((B, Lp, D), F32),
        scratch_shapes=[pltpu.VMEM((ts + CONF_HALO, D), F32), pltpu.VMEM((ts, D), F32)],
        compiler_params=_cparams(("arbitrary", "arbitrary")),
        name="conformer",
    )(*lead, row(ng), w1[:, :D].astype(BF16), w1[:, D:].astype(BF16), row(b1[:D]), row(b1[D:]),
      dww.astype(F32), row(dwb), row(lng), row(lnb), w2.astype(BF16), row(b2))


PROJ_TS = 384


def _proj_kernel(x_ref, g_ref, w_ref, cos_ref, sin_ref, o_ref, *, rope, scale):
    n = _rms(x_ref[...], g_ref[...]).astype(BF16)
    y = _dot(n, w_ref[...])
    if rope:
        cos = cos_ref[...]
        sin = sin_ref[...]
        lane = lax.broadcasted_iota(jnp.int32, cos.shape, 1) % HEAD_DIM
        for cb in range(y.shape[1] // 128):
            yb = y[:, cb * 128:(cb + 1) * 128]
            fwd = pltpu.roll(yb, 128 - ROT_DIM // 2, axis=1)
            bwd = pltpu.roll(yb, ROT_DIM // 2, axis=1)
            partner = jnp.where(lane < ROT_DIM // 2, fwd, bwd)
            o_ref[:, cb * 128:(cb + 1) * 128] = ((yb * cos + partner * sin) * scale).astype(BF16)
    else:
        o_ref[...] = (y * scale).astype(BF16)


def _norm_proj(h2, g, w, cos_t, sin_t, *, rope, scale, lp):
    T, D = h2.shape
    N = w.shape[1]
    ts = PROJ_TS
    nt = lp // ts
    return pl.pallas_call(
        functools.partial(_proj_kernel, rope=rope, scale=scale),
        grid=(T // ts,),
        in_specs=[
            pl.BlockSpec((ts, D), lambda i: (i, 0)),
            pl.BlockSpec((1, D), lambda i: (0, 0)),
            pl.BlockSpec((D, N), lambda i: (0, 0)),
            pl.BlockSpec((ts, 128), lambda i: (i % nt, 0)),
            pl.BlockSpec((ts, 128), lambda i: (i % nt, 0)),
        ],
        out_specs=pl.BlockSpec((ts, N), lambda i: (i, 0)),
        out_shape=jax.ShapeDtypeStruct((T, N), BF16),
        compiler_params=_cparams(("arbitrary",)),
        name="norm_proj",
    )(h2, g.reshape(1, D).astype(F32), w.astype(BF16), cos_t, sin_t)


def _rope_tables(lp):
    inv = ROPE_THETA ** (-jnp.arange(0, ROT_DIM, 2, dtype=F32) / ROT_DIM)
    d = jnp.arange(128) % HEAD_DIM
    ang = jnp.arange(lp, dtype=F32)[:, None] * inv[d % (ROT_DIM // 2)][None, :]
    cos_t = jnp.where(d[None, :] < ROT_DIM, jnp.cos(ang), 1.0)
    sin_t = jnp.where(d[None, :] < ROT_DIM // 2, -jnp.sin(ang),
                      jnp.where(d[None, :] < ROT_DIM, jnp.sin(ang), 0.0))
    return cos_t.astype(F32), sin_t.astype(F32)


ATT_T = 384
ATT_TK = 384
ATT_G = 4


def _attn_t_kernel(qt_ref, k_ref, vt_ref, lam_ref, sgt_ref, o_ref, *, lam_init):
    t = ATT_T
    qi = pl.program_id(2)
    qqs = []
    for g in range(ATT_G):
        qt = qt_ref[0, g, 0]
        row = lax.broadcasted_iota(jnp.int32, qt.shape, 0)
        zero = jnp.zeros_like(qt)
        qqs.append(jnp.concatenate([jnp.where(row < HEAD_DIM, qt, zero),
                                    jnp.where(row >= HEAD_DIM, qt, zero)], axis=1))

    tk = ATT_TK
    sub = t // tk

    def step(kc, carry, masked):
        k0 = pl.multiple_of(kc * tk, tk)
        kk = k_ref[0, pl.ds(k0, tk), :]
        ss = [_dot(kk[:, g * 128:(g + 1) * 128], qqs[g]) for g in range(ATT_G)]
        if masked:
            r = lax.broadcasted_iota(jnp.int32, (tk, 128), 0) + (k0 - qi * t)
            c = lax.broadcasted_iota(jnp.int32, (tk, 128), 1)
        out = []
        for g in range(ATT_G):
            m, l, acc = carry[g]
            ms, ls, ps = [], [], []
            for lb in range(2 * t // 128):
                sl = ss[g][:, lb * 128:(lb + 1) * 128]
                if masked:
                    sl = jnp.where(r <= c + (lb * 128) % t, sl, NEG)
                ml = m[:, lb * 128:(lb + 1) * 128]
                m_new = jnp.maximum(ml, jnp.max(sl, axis=0, keepdims=True))
                ms.append(m_new)
                ls.append(jnp.exp(ml - m_new))
                ps.append(jnp.exp(sl - m_new))
            m_new = jnp.concatenate(ms, axis=1)
            alpha = jnp.concatenate(ls, axis=1)
            p = jnp.concatenate(ps, axis=1)
            l = alpha * l + jnp.sum(p, axis=0, keepdims=True)
            acc = alpha * acc + _dot(vt_ref[0, g, kc], p.astype(BF16))
            out.append((m_new, l, acc))
        return tuple(out)

    init = tuple((jnp.full((1, 2 * t), NEG, F32), jnp.zeros((1, 2 * t), F32),
                  jnp.zeros((V_DIM, 2 * t), F32)) for _ in range(ATT_G))
    carry = lax.fori_loop(0, qi * sub, lambda kc, c: step(kc, c, False), init)
    for d in range(sub):
        carry = step(qi * sub + d, carry, True)
    lp = lam_ref[...]
    lam = (jnp.exp(jnp.sum(lp[0:1] * lp[1:2], axis=-1, keepdims=True))
           - jnp.exp(jnp.sum(lp[2:3] * lp[3:4], axis=-1, keepdims=True)) + lam_init)
    sgt = jnp.concatenate([sgt_ref[...]] * (t // 128), axis=1)
    for g in range(ATT_G):
        m, l, acc = carry[g]
        o = acc / l
        od = o[:, :t] - lam * o[:, t:]
        od = od * lax.rsqrt(jnp.mean(od * od, axis=0, keepdims=True) + EPS)
        od = od * sgt * (1.0 - lam_init)
        o_ref[0, :, g * 128:(g + 1) * 128] = od.T.astype(BF16)


def _diff_attention_t(q3, k3, v3, lam_rows, subln_g, lam_init):
    B, Lp, _ = q3.shape
    t, tk, g = ATT_T, ATT_TK, ATT_G
    nt, ntk = Lp // t, Lp // tk
    to_t = lambda a, rows: a.reshape(B, Lp // rows, rows, N_HEADS, 128).transpose(0, 3, 1, 4, 2)
    sgt = jnp.broadcast_to(subln_g.astype(F32)[:, None], (V_DIM, 128))
    return pl.pallas_call(
        functools.partial(_attn_t_kernel, lam_init=lam_init),
        grid=(B, N_HEADS // g, nt),
        in_specs=[
            pl.BlockSpec((1, g, 1, 128, t), lambda b, h, i: (b, h, i, 0, 0)),
            pl.BlockSpec((1, Lp, 128 * g), lambda b, h, i: (b, 0, h)),
            pl.BlockSpec((1, g, ntk, V_DIM, tk), lambda b, h, i: (b, h, 0, 0, 0)),
            pl.BlockSpec((8, 128), lambda b, h, i: (0, 0)),
            pl.BlockSpec((V_DIM, 128), lambda b, h, i: (0, 0)),
        ],
        out_specs=pl.BlockSpec((1, t, 128 * g), lambda b, h, i: (b, i, h)),
        out_shape=jax.ShapeDtypeStruct((B, Lp, N_HEADS * V_DIM), BF16),
        compiler_params=_cparams(("arbitrary", "arbitrary", "arbitrary")),
        name="diff_attn",
    )(to_t(q3, t), k3, to_t(v3, tk), lam_rows, sgt)


def _matmul_res_kernel(a_ref, w_ref, h_ref, o_ref):
    o_ref[...] = h_ref[...] + _dot(a_ref[...], w_ref[...])


def _matmul_res(a2, w, h2):
    T, K = a2.shape
    N = w.shape[1]
    ts = PROJ_TS
    return pl.pallas_call(
        _matmul_res_kernel,
        grid=(T // ts,),
        in_specs=[pl.BlockSpec((ts, K), lambda i: (i, 0)),
                  pl.BlockSpec((K, N), lambda i: (0, 0)),
                  pl.BlockSpec((ts, N), lambda i: (i, 0))],
        out_specs=pl.BlockSpec((ts, N), lambda i: (i, 0)),
        out_shape=jax.ShapeDtypeStruct((T, N), F32),
        compiler_params=_cparams(("arbitrary",)),
        name="out_proj",
    )(a2, w.astype(BF16), h2)


ROUTE_TS = 256
RANK_SCALE = 2.0 ** 100
RANK_BIAS = 64.0


def _extract_top(s, n):
    rows = lax.broadcasted_iota(jnp.int32, (n, s.shape[1]), 0)
    vals = jnp.full((n, s.shape[1]), NEG, F32)
    work = s
    for j in range(n):
        m = jnp.max(work, axis=0, keepdims=True)
        vals = jnp.where(rows == j, m, vals)
        work = jnp.where(work >= m, -(RANK_BIAS + j) * RANK_SCALE, work)
    rank = jnp.where(work < -0.5 * RANK_BIAS * RANK_SCALE, work * (-1.0 / RANK_SCALE) - RANK_BIAS, float(n))
    return vals, rank


def _top_values(s, n):
    rows = lax.broadcasted_iota(jnp.int32, (n, s.shape[1]), 0)
    vals = jnp.full((n, s.shape[1]), NEG, F32)
    work = s
    for j in range(n):
        m = jnp.max(work, axis=0, keepdims=True)
        vals = jnp.where(rows == j, m, vals)
        if j + 1 < n:
            work = jnp.where(work >= m, NEG, work)
    return vals


def _candidate_sums(v1, v2):
    k = PEER_TOPK
    row8 = lax.broadcasted_iota(jnp.int32, (8, v1.shape[1]), 0)
    blocks = [v1[0:1] + v2]
    for a in range(1, 8):
        nb = k // (a + 1)
        blocks.append(jnp.where(row8 < nb, v1[a:a + 1] + v2[0:8], NEG))
    blocks.append(v1[8:16] + v2[0:1])
    return jnp.concatenate(blocks, axis=0)


def _route_kernel(h_ref, g_ref, wq_ref, sk1_ref, sk2_ref, xn_ref, c_ref, rk_ref, n_ref, r_ref):
    k = PEER_TOPK
    xn = _rms(h_ref[...], g_ref[...]).astype(BF16)
    xn_ref[...] = xn
    q = _dot(xn, wq_ref[...]).astype(BF16)
    for hd in range(PEER_HEADS):
        q1 = q[:, hd * 256:hd * 256 + 128]
        q2 = q[:, hd * 256 + 128:(hd + 1) * 256]
        s1 = _dot_nt(sk1_ref[...], q1)
        s2 = _dot_nt(sk2_ref[...], q2)
        v1, rank1 = _extract_top(s1, k)
        v2, rank2 = _extract_top(s2, k)
        tops = _top_values(_candidate_sums(v1, v2), k)
        z = jnp.sum(jnp.exp(tops - tops[0:1]), axis=0, keepdims=True)
        tau = tops[k - 1:k]
        cnt = jnp.zeros_like(v1)
        for b in range(k):
            cnt = cnt + jnp.where(v1 + v2[b:b + 1] >= tau, 1.0, 0.0)
        rank1_b = rank1.astype(BF16)
        nn = jnp.zeros(rank1_b.shape, BF16)
        for a in range(k):
            row = jnp.broadcast_to(cnt[a:a + 1].astype(BF16), nn.shape)
            nn = jnp.where(rank1_b == a, row, nn)
        n_ref[hd] = nn.astype(F32)
        r_ref[hd] = jnp.exp(s1 - v1[0:1]) / z
        rk_ref[hd] = rank2.astype(BF16)
        c_ref[hd] = jnp.exp(s2 - v2[0:1]).astype(BF16)


def _peer_route(h2, g, wq, sk1, sk2):
    T, D = h2.shape
    ts = ROUTE_TS
    nq = wq.shape[1]
    tab32 = jax.ShapeDtypeStruct((PEER_HEADS, PEER_KEYS, T), F32)
    tab16 = jax.ShapeDtypeStruct((PEER_HEADS, PEER_KEYS, T), BF16)
    tab_spec = pl.BlockSpec((PEER_HEADS, PEER_KEYS, ts), lambda i: (0, 0, i))
    return pl.pallas_call(
        _route_kernel,
        grid=(T // ts,),
        in_specs=[pl.BlockSpec((ts, D), lambda i: (i, 0)),
                  pl.BlockSpec((1, D), lambda i: (0, 0)),
                  pl.BlockSpec((D, nq), lambda i: (0, 0)),
                  pl.BlockSpec((PEER_KEYS, 128), lambda i: (0, 0)),
                  pl.BlockSpec((PEER_KEYS, 128), lambda i: (0, 0))],
        out_specs=[pl.BlockSpec((ts, D), lambda i: (i, 0)), tab_spec, tab_spec, tab_spec, tab_spec],
        out_shape=[jax.ShapeDtypeStruct((T, D), BF16), tab16, tab16, tab32, tab32],
        compiler_params=_cparams(("arbitrary",)),
        name="peer_route",
    )(h2, g.reshape(1, D).astype(F32), wq.astype(BF16), sk1.astype(BF16), sk2.astype(BF16))


PEER_TT = 512
PEER_TC = 256
PEER_ET = 2048
SQRT_HALF = math.sqrt(0.5)


def _fake_zero(block):
    bits = pltpu.bitcast(block, jnp.uint32)
    return pltpu.bitcast((bits >> 16) >> 16, BF16)


def _peer_dense_kernel(xn_ref, u_ref, vt_ref, c_ref, rk_ref, n_ref, r_ref, h_ref, o_ref, acc_ref):
    j = pl.program_id(1)
    tiles_per_step = PEER_ET // PEER_KEYS

    @pl.when(j == 0)
    def _():
        acc_ref[...] = jnp.zeros_like(acc_ref)

    z = _dot_nt(u_ref[...], xn_ref[...])
    pieces = [[] for _ in range(PEER_TT // PEER_TC)]
    w = jnp.zeros((16, PEER_TC), BF16)
    for ii in range(tiles_per_step):
        for tc in range(PEER_TT // PEER_TC):
            tok = slice(tc * PEER_TC, (tc + 1) * PEER_TC)
            nb, rb = [], []
            base8 = pl.multiple_of(j * tiles_per_step + (ii // 8) * 8, 8)
            for hd in range(PEER_HEADS):
                n8 = n_ref[hd, pl.ds(base8, 8), tok]
                r8 = r_ref[hd, pl.ds(base8, 8), tok]
                nb.append(jnp.broadcast_to(n8[ii % 8:ii % 8 + 1], (16, PEER_TC)).astype(BF16))
                rb.append(jnp.broadcast_to(r8[ii % 8:ii % 8 + 1], (16, PEER_TC)).astype(BF16))
            for kb in range(PEER_KEYS // 16):
                rows = slice(kb * 16, (kb + 1) * 16)
                erow = slice(ii * PEER_KEYS + kb * 16, ii * PEER_KEYS + (kb + 1) * 16)
                w = _fake_zero(w)
                for hd in range(PEER_HEADS):
                    w = w + jnp.where(rk_ref[hd, rows, tok] < nb[hd], rb[hd] * c_ref[hd, rows, tok],
                                      jnp.zeros_like(w))
                zz = z[erow, tok]
                act = 0.5 * zz * (1.0 + lax.erf(zz * SQRT_HALF))
                pieces[tc].append(w * act.astype(BF16))
    p = jnp.concatenate([jnp.concatenate(col, axis=0) for col in pieces], axis=1)
    acc_ref[...] += _dot(vt_ref[...], p)

    @pl.when(j == pl.num_programs(1) - 1)
    def _():
        o_ref[...] = h_ref[...] + acc_ref[...].T


def _peer_dense(xn, u_bf, vt_bf, c, rk, n, r, h2):
    T, D = h2.shape
    tt, et = PEER_TT, PEER_ET
    tab_spec = pl.BlockSpec((PEER_HEADS, PEER_KEYS, tt), lambda i, j: (0, 0, i))
    return pl.pallas_call(
        _peer_dense_kernel,
        grid=(T // tt, PEER_EXPERTS // et),
        in_specs=[pl.BlockSpec((tt, D), lambda i, j: (i, 0)),
                  pl.BlockSpec((et, D), lambda i, j: (j, 0)),
                  pl.BlockSpec((D, et), lambda i, j: (0, j)),
                  tab_spec, tab_spec, tab_spec, tab_spec,
                  pl.BlockSpec((tt, D), lambda i, j: (i, 0))],
        out_specs=pl.BlockSpec((tt, D), lambda i, j: (i, 0)),
        out_shape=jax.ShapeDtypeStruct((T, D), F32),
        scratch_shapes=[pltpu.VMEM((D, tt), F32)],
        compiler_params=_cparams(("arbitrary", "arbitrary")),
        name="peer_dense",
    )(xn, u_bf, vt_bf, c, rk, n, r, h2)


def _peer_layer(h2, g, wq, sk1, sk2, u_tab, v_tab):
    xn, c, rk, n, r = _peer_route(h2, g, wq, sk1, sk2)
    return _peer_dense(xn, u_tab.astype(BF16), v_tab.astype(BF16).T, c, rk, n, r, h2)


def _final_norm_kernel(x_ref, g_ref, o_ref):
    o_ref[...] = _rms(x_ref[...], g_ref[...])


def _final_norm(h2, g):
    T, D = h2.shape
    ts = PROJ_TS
    return pl.pallas_call(
        _final_norm_kernel,
        grid=(T // ts,),
        in_specs=[pl.BlockSpec((ts, D), lambda i: (i, 0)), pl.BlockSpec((1, D), lambda i: (0, 0))],
        out_specs=pl.BlockSpec((ts, D), lambda i: (i, 0)),
        out_shape=jax.ShapeDtypeStruct((T, D), F32),
        compiler_params=_cparams(("arbitrary",)),
        name="final_norm",
    )(h2, g.reshape(1, D).astype(F32))


def kernel(x, meta_tokens, a_norm_g, a_pw1_w, a_pw1_b, a_dw_w, a_dw_b, a_ln_g, a_ln_b, a_pw2_w, a_pw2_b, kv_norm_g, w_kv, b_norm_g, b_wq, b_lambda_q1, b_lambda_k1, b_lambda_q2, b_lambda_k2, b_subln_g, b_wo, f_norm_g, f_wq, f_subkey1, f_subkey2, f_u, f_v, final_norm_g):
    B, S, D = x.shape
    L = N_META + S
    Lp = -(-L // BLOCK) * BLOCK
    T = B * Lp
    h = x
    cos_t, sin_t = _rope_tables(Lp)
    nk = 2 * N_HEADS * HEAD_DIM
    k3 = v3 = None
    for layer in range(DEPTH):
        if layer < N_A_LAYERS:
            i = layer
            h = _conformer_layer(h, a_norm_g[i], a_pw1_w[i], a_pw1_b[i], a_dw_w[i], a_dw_b[i],
                                 a_ln_g[i], a_ln_b[i], a_pw2_w[i], a_pw2_b[i],
                                 meta=meta_tokens if layer == 0 else None, lp=Lp)
        else:
            j = layer - N_A_LAYERS
            lam_init = 0.8 - 0.6 * math.exp(-0.3 * layer)
            h2 = h.reshape(T, D)
            q = _norm_proj(h2, b_norm_g[j], b_wq[j], cos_t, sin_t, rope=True,
                           scale=HEAD_DIM ** -0.5, lp=Lp)
            lam_rows = jnp.zeros((8, 128), F32).at[0:4, 0:HEAD_DIM].set(
                jnp.stack([b_lambda_q1[j], b_lambda_k1[j], b_lambda_q2[j], b_lambda_k2[j]]).astype(F32))
            o = _diff_attention_t(q.reshape(B, Lp, nk), k3, v3, lam_rows, b_subln_g[j], lam_init)
            h = _matmul_res(o.reshape(T, N_HEADS * V_DIM), b_wo[j], h2).reshape(B, Lp, D)
        h2 = _peer_layer(h.reshape(T, D), f_norm_g[layer], f_wq[layer], f_subkey1[layer],
                         f_subkey2[layer], f_u[layer], f_v[layer])
        h = h2.reshape(B, Lp, D)
        if layer == N_A_LAYERS - 1:
            k3 = _norm_proj(h2, kv_norm_g, w_kv[:, :nk], cos_t, sin_t, rope=True, scale=1.0,
                            lp=Lp).reshape(B, Lp, nk)
            v3 = _norm_proj(h2, kv_norm_g, w_kv[:, nk:], cos_t, sin_t, rope=False, scale=1.0,
                            lp=Lp).reshape(B, Lp, N_HEADS * V_DIM)
    out = _final_norm(h.reshape(T, D), final_norm_g).reshape(B, Lp, D)
    return out[:, N_META:N_META + S]
```

```python
import functools
import math

import jax
import jax.numpy as jnp
from jax import lax
from jax.experimental import pallas as pl
from jax.experimental.pallas import tpu as pltpu

D_MODEL = 1024
N_META = 16
BLOCK = 128
DEPTH = 4
N_A_LAYERS = DEPTH // 2
CONV_WIDTH = 31
N_HEADS = 8
HEAD_DIM = 64
V_DIM = 128
ROT_DIM = 16
ROPE_THETA = 500000.0
PEER_HEADS = 8
PEER_KEYS = 128
PEER_EXPERTS = PEER_KEYS * PEER_KEYS
PEER_TOPK = 16
EPS = 1e-6

NEG = -1e30
VMEM_LIMIT = 52 * 1024 * 1024

F32 = jnp.float32
BF16 = jnp.bfloat16


def _cparams(sem, flags=None):
    return pltpu.CompilerParams(dimension_semantics=sem, vmem_limit_bytes=VMEM_LIMIT, flags=flags)


def _rms(x, g):
    return x * lax.rsqrt(jnp.mean(x * x, axis=-1, keepdims=True) + EPS) * g


def _dot(a, b):
    return jnp.dot(a, b, preferred_element_type=F32)


def _dot_nt(a, b):
    return lax.dot_general(a, b, (((1,), (1,)), ((), ())), preferred_element_type=F32)


CONF_TS = 384
CONF_HALO = 32
CONF_RB = 32
CONF_CB = 256


def _conformer_kernel(x_ref, *rest):
    _conformer_body(x_ref[0], *rest)


def _conformer_first_kernel(xp_ref, xc_ref, meta_ref, *rest, seq):
    ts = CONF_TS
    t = pl.program_id(1)
    x = jnp.concatenate([xp_ref[0, ts - N_META:ts, :], xc_ref[0, 0:ts - N_META, :]], axis=0)
    grow = t * ts + lax.broadcasted_iota(jnp.int32, (ts, 1), 0)
    meta = jnp.concatenate([meta_ref[...], jnp.zeros((ts - N_META, D_MODEL), F32)], axis=0)
    x = jnp.where(grow < N_META, meta, jnp.where(grow < N_META + seq, x, 0.0))
    _conformer_body(x, *rest)


def _conformer_body(x, ng_ref, w1a_ref, w1g_ref, b1a_ref, b1g_ref, dww_ref, dwb_ref,
                    lng_ref, lnb_ref, w2_ref, b2_ref, o_ref, buf_ref, cbuf_ref):
    ts = CONF_TS

    @pl.when(pl.program_id(1) == 0)
    def _():
        buf_ref[0:CONF_HALO, :] = jnp.zeros((CONF_HALO, D_MODEL), F32)

    n = _rms(x, ng_ref[...]).astype(BF16)
    a = _dot(n, w1a_ref[...]) + b1a_ref[...]
    g = _dot(n, w1g_ref[...]) + b1g_ref[...]
    buf_ref[CONF_HALO:CONF_HALO + ts, :] = a * jax.nn.sigmoid(g)

    off0 = CONF_HALO - (CONV_WIDTH - 1)

    def row_body(rb, carry):
        r0 = pl.multiple_of(rb * CONF_RB, CONF_RB)
        for cb in range(D_MODEL // CONF_CB):
            cols = slice(cb * CONF_CB, (cb + 1) * CONF_CB)
            acc = jnp.broadcast_to(dwb_ref[:, cols], (CONF_RB, CONF_CB))
            win = buf_ref[pl.ds(r0, CONF_RB + CONF_HALO), cols]
            shifted = [win] + [pltpu.roll(win, CONF_RB + CONF_HALO - j, axis=0) for j in range(1, 8)]
            for k in range(CONV_WIDTH):
                q8, j = divmod(off0 + k, 8)
                acc = acc + dww_ref[k:k + 1, cols] * shifted[j][8 * q8:8 * q8 + CONF_RB]
            cbuf_ref[pl.ds(r0, CONF_RB), cols] = acc
        return carry

    lax.fori_loop(0, ts // CONF_RB, row_body, 0)
    buf_ref[0:CONF_HALO, :] = buf_ref[ts:ts + CONF_HALO, :]

    c = cbuf_ref[...]
    mu = jnp.mean(c, axis=-1, keepdims=True)
    cc = c - mu
    y = cc * lax.rsqrt(jnp.mean(cc * cc, axis=-1, keepdims=True) + EPS)
    y = y * lng_ref[...] + lnb_ref[...]
    y = (y * jax.nn.sigmoid(y)).astype(BF16)
    o_ref[0] = x + _dot(y, w2_ref[...]) + b2_ref[...]


def _conformer_layer(h3, ng, w1, b1, dww, dwb, lng, lnb, w2, b2, *, meta=None, lp=None):
    B, _, D = h3.shape
    Lp = h3.shape[1] if meta is None else lp
    ts = CONF_TS
    row = lambda v: v.reshape(1, -1).astype(F32)
    const = lambda shape: pl.BlockSpec(shape, lambda b, t: (0,) * len(shape))
    if meta is None:
        body, lead_specs, lead = _conformer_kernel, [pl.BlockSpec((1, ts, D), lambda b, t: (b, t, 0))], (h3,)
    else:
        last = (h3.shape[1] - 1) // ts
        body = functools.partial(_conformer_first_kernel, seq=h3.shape[1])
        lead_specs = [pl.BlockSpec((1, ts, D), lambda b, t: (b, jnp.maximum(t - 1, 0), 0)),
                      pl.BlockSpec((1, ts, D), lambda b, t: (b, jnp.minimum(t, last), 0)),
                      const((N_META, D))]
        lead = (h3, h3, meta.astype(F32))
    return pl.pallas_call(
        body,
        grid=(B, Lp // ts),
        in_specs=lead_specs + [
            const((1, D)), const((D, D)), const((D, D)), const((1, D)), const((1, D)),
            const((CONV_WIDTH, D)), const((1, D)), const((1, D)), const((1, D)),
            const((D, D)), const((1, D)),
        ],
        out_specs=pl.BlockSpec((1, ts, D), lambda b, t: (b, t, 0)),
        out_shape=jax.ShapeDtypeStruct((B, Lp, D), F32),
        scratch_shapes=[pltpu.VMEM((ts + CONF_HALO, D), F32), pltpu.VMEM((ts, D), F32)],
        compiler_params=_cparams(("arbitrary", "arbitrary")),
        name="conformer",
    )(*lead, row(ng), w1[:, :D].astype(BF16), w1[:, D:].astype(BF16), row(b1[:D]), row(b1[D:]),
      dww.astype(F32), row(dwb), row(lng), row(lnb), w2.astype(BF16), row(b2))


PROJ_TS = 384


def _proj_kernel(x_ref, g_ref, w_ref, cos_ref, sin_ref, o_ref, *, rope, scale, head_major):
    n = _rms(x_ref[...], g_ref[...]).astype(BF16)
    y = _dot(n, w_ref[...])
    if rope:
        cos = cos_ref[...]
        sin = sin_ref[...]
        lane = lax.broadcasted_iota(jnp.int32, cos.shape, 1) % HEAD_DIM
    for cb in range(y.shape[1] // 128):
        yb = y[:, cb * 128:(cb + 1) * 128]
        if rope:
            fwd = pltpu.roll(yb, 128 - ROT_DIM // 2, axis=1)
            bwd = pltpu.roll(yb, ROT_DIM // 2, axis=1)
            partner = jnp.where(lane < ROT_DIM // 2, fwd, bwd)
            yb = yb * cos + partner * sin
        if head_major:
            o_ref[0, cb, 0] = (yb * scale).T.astype(BF16)
        else:
            o_ref[:, cb * 128:(cb + 1) * 128] = (yb * scale).astype(BF16)


def _norm_proj(h2, g, w, cos_t, sin_t, *, rope, scale, lp, head_major=False):
    T, D = h2.shape
    N = w.shape[1]
    ts = PROJ_TS
    nt = lp // ts
    if head_major:
        out_spec = pl.BlockSpec((1, N // 128, 1, 128, ts), lambda i: (i // nt, 0, i % nt, 0, 0))
        out_shape = jax.ShapeDtypeStruct((T // lp, N // 128, nt, 128, ts), BF16)
    else:
        out_spec = pl.BlockSpec((ts, N), lambda i: (i, 0))
        out_shape = jax.ShapeDtypeStruct((T, N), BF16)
    return pl.pallas_call(
        functools.partial(_proj_kernel, rope=rope, scale=scale, head_major=head_major),
        grid=(T // ts,),
        in_specs=[
            pl.BlockSpec((ts, D), lambda i: (i, 0)),
            pl.BlockSpec((1, D), lambda i: (0, 0)),
            pl.BlockSpec((D, N), lambda i: (0, 0)),
            pl.BlockSpec((ts, 128), lambda i: (i % nt, 0)),
            pl.BlockSpec((ts, 128), lambda i: (i % nt, 0)),
        ],
        out_specs=out_spec,
        out_shape=out_shape,
        compiler_params=_cparams(("arbitrary",)),
        name="norm_proj",
    )(h2, g.reshape(1, D).astype(F32), w.astype(BF16), cos_t, sin_t)


def _rope_tables(lp):
    inv = ROPE_THETA ** (-jnp.arange(0, ROT_DIM, 2, dtype=F32) / ROT_DIM)
    d = jnp.arange(128) % HEAD_DIM
    ang = jnp.arange(lp, dtype=F32)[:, None] * inv[d % (ROT_DIM // 2)][None, :]
    cos_t = jnp.where(d[None, :] < ROT_DIM, jnp.cos(ang), 1.0)
    sin_t = jnp.where(d[None, :] < ROT_DIM // 2, -jnp.sin(ang),
                      jnp.where(d[None, :] < ROT_DIM, jnp.sin(ang), 0.0))
    return cos_t.astype(F32), sin_t.astype(F32)


ATT_T = 384
ATT_TK = 384
ATT_G = 4


def _attn_t_kernel(qt_ref, k_ref, vt_ref, lam_ref, sgt_ref, o_ref, *, lam_init):
    t = ATT_T
    qi = pl.program_id(2)
    qqs = []
    for g in range(ATT_G):
        qt = qt_ref[0, g, 0]
        row = lax.broadcasted_iota(jnp.int32, qt.shape, 0)
        zero = jnp.zeros_like(qt)
        qqs.append(jnp.concatenate([jnp.where(row < HEAD_DIM, qt, zero),
                                    jnp.where(row >= HEAD_DIM, qt, zero)], axis=1))

    tk = ATT_TK
    sub = t // tk

    def step(kc, carry, masked):
        k0 = pl.multiple_of(kc * tk, tk)
        kk = k_ref[0, pl.ds(k0, tk), :]
        ss = [_dot(kk[:, g * 128:(g + 1) * 128], qqs[g]) for g in range(ATT_G)]
        if masked:
            r = lax.broadcasted_iota(jnp.int32, (tk, 128), 0) + (k0 - qi * t)
            c = lax.broadcasted_iota(jnp.int32, (tk, 128), 1)
        out = []
        for g in range(ATT_G):
            m, l, acc = carry[g]
            ms, ls, ps = [], [], []
            for lb in range(2 * t // 128):
                sl = ss[g][:, lb * 128:(lb + 1) * 128]
                if masked:
                    sl = jnp.where(r <= c + (lb * 128) % t, sl, NEG)
                ml = m[:, lb * 128:(lb + 1) * 128]
                m_new = jnp.maximum(ml, jnp.max(sl, axis=0, keepdims=True))
                ms.append(m_new)
                ls.append(jnp.exp(ml - m_new))
                ps.append(jnp.exp(sl - m_new))
            m_new = jnp.concatenate(ms, axis=1)
            alpha = jnp.concatenate(ls, axis=1)
            p = jnp.concatenate(ps, axis=1)
            l = alpha * l + jnp.sum(p, axis=0, keepdims=True)
            acc = alpha * acc + _dot(vt_ref[0, g, kc], p.astype(BF16))
            out.append((m_new, l, acc))
        return tuple(out)

    init = tuple((jnp.full((1, 2 * t), NEG, F32), jnp.zeros((1, 2 * t), F32),
                  jnp.zeros((V_DIM, 2 * t), F32)) for _ in range(ATT_G))
    carry = lax.fori_loop(0, qi * sub, lambda kc, c: step(kc, c, False), init)
    for d in range(sub):
        carry = step(qi * sub + d, carry, True)
    lp = lam_ref[...]
    lam = (jnp.exp(jnp.sum(lp[0:1] * lp[1:2], axis=-1, keepdims=True))
           - jnp.exp(jnp.sum(lp[2:3] * lp[3:4], axis=-1, keepdims=True)) + lam_init)
    sgt = jnp.concatenate([sgt_ref[...]] * (t // 128), axis=1)
    for g in range(ATT_G):
        m, l, acc = carry[g]
        o = acc / l
        od = o[:, :t] - lam * o[:, t:]
        od = od * lax.rsqrt(jnp.mean(od * od, axis=0, keepdims=True) + EPS)
        od = od * sgt * (1.0 - lam_init)
        o_ref[0, :, g * 128:(g + 1) * 128] = od.T.astype(BF16)


def _diff_attention_t(qt, k3, vt, lam_rows, subln_g, lam_init):
    B, Lp, _ = k3.shape
    t, tk, g = ATT_T, ATT_TK, ATT_G
    nt, ntk = Lp // t, Lp // tk
    assert qt.shape == (B, N_HEADS, nt, 128, t) and vt.shape == (B, N_HEADS, ntk, V_DIM, tk)
    sgt = jnp.broadcast_to(subln_g.astype(F32)[:, None], (V_DIM, 128))
    return pl.pallas_call(
        functools.partial(_attn_t_kernel, lam_init=lam_init),
        grid=(B, N_HEADS // g, nt),
        in_specs=[
            pl.BlockSpec((1, g, 1, 128, t), lambda b, h, i: (b, h, i, 0, 0)),
            pl.BlockSpec((1, Lp, 128 * g), lambda b, h, i: (b, 0, h)),
            pl.BlockSpec((1, g, ntk, V_DIM, tk), lambda b, h, i: (b, h, 0, 0, 0)),
            pl.BlockSpec((8, 128), lambda b, h, i: (0, 0)),
            pl.BlockSpec((V_DIM, 128), lambda b, h, i: (0, 0)),
        ],
        out_specs=pl.BlockSpec((1, t, 128 * g), lambda b, h, i: (b, i, h)),
        out_shape=jax.ShapeDtypeStruct((B, Lp, N_HEADS * V_DIM), BF16),
        compiler_params=_cparams(("arbitrary", "arbitrary", "arbitrary")),
        name="diff_attn",
    )(qt, k3, vt, lam_rows, sgt)


def _matmul_res_kernel(a_ref, w_ref, h_ref, o_ref):
    o_ref[...] = h_ref[...] + _dot(a_ref[...], w_ref[...])


def _matmul_res(a2, w, h2):
    T, K = a2.shape
    N = w.shape[1]
    ts = PROJ_TS
    return pl.pallas_call(
        _matmul_res_kernel,
        grid=(T // ts,),
        in_specs=[pl.BlockSpec((ts, K), lambda i: (i, 0)),
                  pl.BlockSpec((K, N), lambda i: (0, 0)),
                  pl.BlockSpec((ts, N), lambda i: (i, 0))],
        out_specs=pl.BlockSpec((ts, N), lambda i: (i, 0)),
        out_shape=jax.ShapeDtypeStruct((T, N), F32),
        compiler_params=_cparams(("arbitrary",)),
        name="out_proj",
    )(a2, w.astype(BF16), h2)


ROUTE_TS = 256
RANK_SCALE = 2.0 ** 100
RANK_BIAS = 64.0


def _extract_top(s, n):
    rows = lax.broadcasted_iota(jnp.int32, (n, s.shape[1]), 0)
    vals = jnp.full((n, s.shape[1]), NEG, F32)
    work = s
    for j in range(n):
        m = jnp.max(work, axis=0, keepdims=True)
        vals = jnp.where(rows == j, m, vals)
        work = jnp.where(work >= m, -(RANK_BIAS + j) * RANK_SCALE, work)
    rank = jnp.where(work < -0.5 * RANK_BIAS * RANK_SCALE, work * (-1.0 / RANK_SCALE) - RANK_BIAS, float(n))
    return vals, rank


def _top_values(s, n):
    rows = lax.broadcasted_iota(jnp.int32, (n, s.shape[1]), 0)
    vals = jnp.full((n, s.shape[1]), NEG, F32)
    work = s
    for j in range(n):
        m = jnp.max(work, axis=0, keepdims=True)
        vals = jnp.where(rows == j, m, vals)
        if j + 1 < n:
            work = jnp.where(work >= m, NEG, work)
    return vals


def _candidate_sums(v1, v2):
    k = PEER_TOPK
    row8 = lax.broadcasted_iota(jnp.int32, (8, v1.shape[1]), 0)
    blocks = [v1[0:1] + v2]
    for a in range(1, 8):
        nb = k // (a + 1)
        blocks.append(jnp.where(row8 < nb, v1[a:a + 1] + v2[0:8], NEG))
    blocks.append(v1[8:16] + v2[0:1])
    return jnp.concatenate(blocks, axis=0)


def _route_kernel(h_ref, g_ref, wq_ref, sk1_ref, sk2_ref, xn_ref, c_ref, rk_ref, n_ref, r_ref):
    k = PEER_TOPK
    xn = _rms(h_ref[...], g_ref[...]).astype(BF16)
    xn_ref[...] = xn
    q = _dot(xn, wq_ref[...]).astype(BF16)
    for hd in range(PEER_HEADS):
        q1 = q[:, hd * 256:hd * 256 + 128]
        q2 = q[:, hd * 256 + 128:(hd + 1) * 256]
        s1 = _dot_nt(sk1_ref[...], q1)
        s2 = _dot_nt(sk2_ref[...], q2)
        v1, rank1 = _extract_top(s1, k)
        v2, rank2 = _extract_top(s2, k)
        tops = _top_values(_candidate_sums(v1, v2), k)
        z = jnp.sum(jnp.exp(tops - tops[0:1]), axis=0, keepdims=True)
        tau = tops[k - 1:k]
        cnt = jnp.zeros_like(v1)
        for b in range(k):
            cnt = cnt + jnp.where(v1 + v2[b:b + 1] >= tau, 1.0, 0.0)
        rank1_b = rank1.astype(BF16)
        nn = jnp.zeros(rank1_b.shape, BF16)
        for a in range(k):
            row = jnp.broadcast_to(cnt[a:a + 1].astype(BF16), nn.shape)
            nn = jnp.where(rank1_b == a, row, nn)
        n_ref[hd] = nn.astype(F32)
        r_ref[hd] = jnp.exp(s1 - v1[0:1]) / z
        rk_ref[hd] = rank2.astype(BF16)
        c_ref[hd] = jnp.exp(s2 - v2[0:1]).astype(BF16)


def _peer_route(h2, g, wq, sk1, sk2):
    T, D = h2.shape
    ts = ROUTE_TS
    nq = wq.shape[1]
    tab32 = jax.ShapeDtypeStruct((PEER_HEADS, PEER_KEYS, T), F32)
    tab16 = jax.ShapeDtypeStruct((PEER_HEADS, PEER_KEYS, T), BF16)
    tab_spec = pl.BlockSpec((PEER_HEADS, PEER_KEYS, ts), lambda i: (0, 0, i))
    return pl.pallas_call(
        _route_kernel,
        grid=(T // ts,),
        in_specs=[pl.BlockSpec((ts, D), lambda i: (i, 0)),
                  pl.BlockSpec((1, D), lambda i: (0, 0)),
                  pl.BlockSpec((D, nq), lambda i: (0, 0)),
                  pl.BlockSpec((PEER_KEYS, 128), lambda i: (0, 0)),
                  pl.BlockSpec((PEER_KEYS, 128), lambda i: (0, 0))],
        out_specs=[pl.BlockSpec((ts, D), lambda i: (i, 0)), tab_spec, tab_spec, tab_spec, tab_spec],
        out_shape=[jax.ShapeDtypeStruct((T, D), BF16), tab16, tab16, tab32, tab32],
        compiler_params=_cparams(("arbitrary",)),
        name="peer_route",
    )(h2, g.reshape(1, D).astype(F32), wq.astype(BF16), sk1.astype(BF16), sk2.astype(BF16))


PEER_TT = 512
PEER_TC = 256
PEER_ET = 2048
SQRT_HALF = math.sqrt(0.5)


def _fake_zero(block):
    bits = pltpu.bitcast(block, jnp.uint32)
    return pltpu.bitcast((bits >> 16) >> 16, BF16)


def _peer_dense_kernel(xn_ref, u_ref, vt_ref, c_ref, rk_ref, n_ref, r_ref, h_ref, o_ref, acc_ref):
    j = pl.program_id(1)
    tiles_per_step = PEER_ET // PEER_KEYS

    @pl.when(j == 0)
    def _():
        acc_ref[...] = jnp.zeros_like(acc_ref)

    z = _dot_nt(u_ref[...], xn_ref[...])
    pieces = [[] for _ in range(PEER_TT // PEER_TC)]
    w = jnp.zeros((16, PEER_TC), BF16)
    for ii in range(tiles_per_step):
        for tc in range(PEER_TT // PEER_TC):
            tok = slice(tc * PEER_TC, (tc + 1) * PEER_TC)
            nb, rb = [], []
            base8 = pl.multiple_of(j * tiles_per_step + (ii // 8) * 8, 8)
            for hd in range(PEER_HEADS):
                n8 = n_ref[hd, pl.ds(base8, 8), tok]
                r8 = r_ref[hd, pl.ds(base8, 8), tok]
                nb.append(jnp.broadcast_to(n8[ii % 8:ii % 8 + 1], (16, PEER_TC)).astype(BF16))
                rb.append(jnp.broadcast_to(r8[ii % 8:ii % 8 + 1], (16, PEER_TC)).astype(BF16))
            for kb in range(PEER_KEYS // 16):
                rows = slice(kb * 16, (kb + 1) * 16)
                erow = slice(ii * PEER_KEYS + kb * 16, ii * PEER_KEYS + (kb + 1) * 16)
                w = _fake_zero(w)
                for hd in range(PEER_HEADS):
                    w = w + jnp.where(rk_ref[hd, rows, tok] < nb[hd], rb[hd] * c_ref[hd, rows, tok],
                                      jnp.zeros_like(w))
                zz = z[erow, tok]
                act = 0.5 * zz * (1.0 + lax.erf(zz * SQRT_HALF))
                pieces[tc].append(w * act.astype(BF16))
    p = jnp.concatenate([jnp.concatenate(col, axis=0) for col in pieces], axis=1)
    acc_ref[...] += _dot(vt_ref[...], p)

    @pl.when(j == pl.num_programs(1) - 1)
    def _():
        o_ref[...] = h_ref[...] + acc_ref[...].T


def _peer_dense(xn, u_bf, vt_bf, layer, c, rk, n, r, h2):
    T, D = h2.shape
    tt, et = PEER_TT, PEER_ET
    tab_spec = pl.BlockSpec((PEER_HEADS, PEER_KEYS, tt), lambda i, j: (0, 0, i))
    return pl.pallas_call(
        _peer_dense_kernel,
        grid=(T // tt, PEER_EXPERTS // et),
        in_specs=[pl.BlockSpec((tt, D), lambda i, j: (i, 0)),
                  pl.BlockSpec((None, et, D), lambda i, j: (layer, j, 0)),
                  pl.BlockSpec((None, D, et), lambda i, j: (layer, 0, j)),
                  tab_spec, tab_spec, tab_spec, tab_spec,
                  pl.BlockSpec((tt, D), lambda i, j: (i, 0))],
        out_specs=pl.BlockSpec((tt, D), lambda i, j: (i, 0)),
        out_shape=jax.ShapeDtypeStruct((T, D), F32),
        scratch_shapes=[pltpu.VMEM((D, tt), F32)],
        compiler_params=_cparams(("arbitrary", "arbitrary")),
        name="peer_dense",
    )(xn, u_bf, vt_bf, c, rk, n, r, h2)


def _peer_layer(h2, g, wq, sk1, sk2, u_bf, vt_bf, layer):
    xn, c, rk, n, r = _peer_route(h2, g, wq, sk1, sk2)
    return _peer_dense(xn, u_bf, vt_bf, layer, c, rk, n, r, h2)


def _final_norm_kernel(x_ref, g_ref, o_ref):
    o_ref[...] = _rms(x_ref[...], g_ref[...])


def _final_norm(h2, g):
    T, D = h2.shape
    ts = PROJ_TS
    return pl.pallas_call(
        _final_norm_kernel,
        grid=(T // ts,),
        in_specs=[pl.BlockSpec((ts, D), lambda i: (i, 0)), pl.BlockSpec((1, D), lambda i: (0, 0))],
        out_specs=pl.BlockSpec((ts, D), lambda i: (i, 0)),
        out_shape=jax.ShapeDtypeStruct((T, D), F32),
        compiler_params=_cparams(("arbitrary",)),
        name="final_norm",
    )(h2, g.reshape(1, D).astype(F32))


def kernel(x, meta_tokens, a_norm_g, a_pw1_w, a_pw1_b, a_dw_w, a_dw_b, a_ln_g, a_ln_b, a_pw2_w, a_pw2_b, kv_norm_g, w_kv, b_norm_g, b_wq, b_lambda_q1, b_lambda_k1, b_lambda_q2, b_lambda_k2, b_subln_g, b_wo, f_norm_g, f_wq, f_subkey1, f_subkey2, f_u, f_v, final_norm_g):
    B, S, D = x.shape
    L = N_META + S
    Lp = -(-L // BLOCK) * BLOCK
    T = B * Lp
    h = x
    u_bf = f_u.astype(BF16)
    vt_bf = jnp.swapaxes(f_v.astype(BF16), 1, 2)
    cos_t, sin_t = _rope_tables(Lp)
    nk = 2 * N_HEADS * HEAD_DIM
    k3 = vt = None
    for layer in range(DEPTH):
        if layer < N_A_LAYERS:
            i = layer
            h = _conformer_layer(h, a_norm_g[i], a_pw1_w[i], a_pw1_b[i], a_dw_w[i], a_dw_b[i],
                                 a_ln_g[i], a_ln_b[i], a_pw2_w[i], a_pw2_b[i],
                                 meta=meta_tokens if layer == 0 else None, lp=Lp)
        else:
            j = layer - N_A_LAYERS
            lam_init = 0.8 - 0.6 * math.exp(-0.3 * layer)
            h2 = h.reshape(T, D)
            qt = _norm_proj(h2, b_norm_g[j], b_wq[j], cos_t, sin_t, rope=True,
                            scale=HEAD_DIM ** -0.5, lp=Lp, head_major=True)
            lam_rows = jnp.zeros((8, 128), F32).at[0:4, 0:HEAD_DIM].set(
                jnp.stack([b_lambda_q1[j], b_lambda_k1[j], b_lambda_q2[j], b_lambda_k2[j]]).astype(F32))
            o = _diff_attention_t(qt, k3, vt, lam_rows, b_subln_g[j], lam_init)
            h = _matmul_res(o.reshape(T, N_HEADS * V_DIM), b_wo[j], h2).reshape(B, Lp, D)
        h2 = _peer_layer(h.reshape(T, D), f_norm_g[layer], f_wq[layer], f_subkey1[layer],
                         f_subkey2[layer], u_bf, vt_bf, layer)
        h = h2.reshape(B, Lp, D)
        if layer == N_A_LAYERS - 1:
            k3 = _norm_proj(h2, kv_norm_g, w_kv[:, :nk], cos_t, sin_t, rope=True, scale=1.0,
                            lp=Lp).reshape(B, Lp, nk)
            vt = _norm_proj(h2, kv_norm_g, w_kv[:, nk:], cos_t, sin_t, rope=False, scale=1.0,
                            lp=Lp, head_major=True)
    out = _final_norm(h.reshape(T, D), final_norm_g).reshape(B, Lp, D)
    return out[:, N_META:N_META + S]
```

```python
import functools
import math

import jax
import jax.numpy as jnp
from jax import lax
from jax.experimental import pallas as pl
from jax.experimental.pallas import tpu as pltpu

D_MODEL = 1024
N_META = 16
BLOCK = 128
DEPTH = 4
N_A_LAYERS = DEPTH // 2
CONV_WIDTH = 31
N_HEADS = 8
HEAD_DIM = 64
V_DIM = 128
ROT_DIM = 16
ROPE_THETA = 500000.0
PEER_HEADS = 8
PEER_KEYS = 128
PEER_EXPERTS = PEER_KEYS * PEER_KEYS
PEER_TOPK = 16
EPS = 1e-6

NEG = -1e30
VMEM_LIMIT = 52 * 1024 * 1024

F32 = jnp.float32
BF16 = jnp.bfloat16


def _cparams(sem, flags=None):
    return pltpu.CompilerParams(dimension_semantics=sem, vmem_limit_bytes=VMEM_LIMIT, flags=flags)


def _rms(x, g):
    return x * lax.rsqrt(jnp.mean(x * x, axis=-1, keepdims=True) + EPS) * g


def _dot(a, b):
    return jnp.dot(a, b, preferred_element_type=F32)


def _dot_nt(a, b):
    return lax.dot_general(a, b, (((1,), (1,)), ((), ())), preferred_element_type=F32)


CONF_TS = 384
CONF_HALO = 32
CONF_RB = 32
CONF_CB = 256


def _conformer_kernel(x_ref, *rest):
    _conformer_body(x_ref[0], *rest)


def _conformer_first_kernel(xp_ref, xc_ref, meta_ref, *rest, seq):
    ts = CONF_TS
    t = pl.program_id(1)
    x = jnp.concatenate([xp_ref[0, ts - N_META:ts, :], xc_ref[0, 0:ts - N_META, :]], axis=0)
    grow = t * ts + lax.broadcasted_iota(jnp.int32, (ts, 1), 0)
    meta = jnp.concatenate([meta_ref[...], jnp.zeros((ts - N_META, D_MODEL), F32)], axis=0)
    x = jnp.where(grow < N_META, meta, jnp.where(grow < N_META + seq, x, 0.0))
    _conformer_body(x, *rest)


def _conformer_body(x, ng_ref, w1a_ref, w1g_ref, b1a_ref, b1g_ref, dww_ref, dwb_ref,
                    lng_ref, lnb_ref, w2_ref, b2_ref, o_ref, buf_ref, cbuf_ref):
    ts = CONF_TS

    @pl.when(pl.program_id(1) == 0)
    def _():
        buf_ref[0:CONF_HALO, :] = jnp.zeros((CONF_HALO, D_MODEL), F32)

    n = _rms(x, ng_ref[...]).astype(BF16)
    a = _dot(n, w1a_ref[...]) + b1a_ref[...]
    g = _dot(n, w1g_ref[...]) + b1g_ref[...]
    buf_ref[CONF_HALO:CONF_HALO + ts, :] = a * jax.nn.sigmoid(g)

    off0 = CONF_HALO - (CONV_WIDTH - 1)

    def row_body(rb, carry):
        r0 = pl.multiple_of(rb * CONF_RB, CONF_RB)
        for cb in range(D_MODEL // CONF_CB):
            cols = slice(cb * CONF_CB, (cb + 1) * CONF_CB)
            acc = jnp.broadcast_to(dwb_ref[:, cols], (CONF_RB, CONF_CB))
            win = buf_ref[pl.ds(r0, CONF_RB + CONF_HALO), cols]
            shifted = [win] + [pltpu.roll(win, CONF_RB + CONF_HALO - j, axis=0) for j in range(1, 8)]
            for k in range(CONV_WIDTH):
                q8, j = divmod(off0 + k, 8)
                acc = acc + dww_ref[k:k + 1, cols] * shifted[j][8 * q8:8 * q8 + CONF_RB]
            cbuf_ref[pl.ds(r0, CONF_RB), cols] = acc
        return carry

    lax.fori_loop(0, ts // CONF_RB, row_body, 0)
    buf_ref[0:CONF_HALO, :] = buf_ref[ts:ts + CONF_HALO, :]

    c = cbuf_ref[...]
    mu = jnp.mean(c, axis=-1, keepdims=True)
    cc = c - mu
    y = cc * lax.rsqrt(jnp.mean(cc * cc, axis=-1, keepdims=True) + EPS)
    y = y * lng_ref[...] + lnb_ref[...]
    y = (y * jax.nn.sigmoid(y)).astype(BF16)
    o_ref[0] = x + _dot(y, w2_ref[...]) + b2_ref[...]


def _conformer_layer(h3, ng, w1, b1, dww, dwb, lng, lnb, w2, b2, *, meta=None, lp=None):
    B, _, D = h3.shape
    Lp = h3.shape[1] if meta is None else lp
    ts = CONF_TS
    row = lambda v: v.reshape(1, -1).astype(F32)
    const = lambda shape: pl.BlockSpec(shape, lambda b, t: (0,) * len(shape))
    if meta is None:
        body, lead_specs, lead = _conformer_kernel, [pl.BlockSpec((1, ts, D), lambda b, t: (b, t, 0))], (h3,)
    else:
        last = (h3.shape[1] - 1) // ts
        body = functools.partial(_conformer_first_kernel, seq=h3.shape[1])
        lead_specs = [pl.BlockSpec((1, ts, D), lambda b, t: (b, jnp.maximum(t - 1, 0), 0)),
                      pl.BlockSpec((1, ts, D), lambda b, t: (b, jnp.minimum(t, last), 0)),
                      const((N_META, D))]
        lead = (h3, h3, meta.astype(F32))
    return pl.pallas_call(
        body,
        grid=(B, Lp // ts),
        in_specs=lead_specs + [
            const((1, D)), const((D, D)), const((D, D)), const((1, D)), const((1, D)),
            const((CONV_WIDTH, D)), const((1, D)), const((1, D)), const((1, D)),
            const((D, D)), const((1, D)),
        ],
        out_specs=pl.BlockSpec((1, ts, D), lambda b, t: (b, t, 0)),
        out_shape=jax.ShapeDtypeStruct((B, Lp, D), F32),
        scratch_shapes=[pltpu.VMEM((ts + CONF_HALO, D), F32), pltpu.VMEM((ts, D), F32)],
        compiler_params=_cparams(("arbitrary", "arbitrary")),
        name="conformer",
    )(*lead, row(ng), w1[:, :D].astype(BF16), w1[:, D:].astype(BF16), row(b1[:D]), row(b1[D:]),
      dww.astype(F32), row(dwb), row(lng), row(lnb), w2.astype(BF16), row(b2))


PROJ_TS = 384


def _proj_kernel(x_ref, g_ref, w_ref, cos_ref, sin_ref, o_ref, *, rope, scale, head_major):
    n = _rms(x_ref[...], g_ref[...]).astype(BF16)
    y = _dot(n, w_ref[...])
    if rope:
        cos = cos_ref[...]
        sin = sin_ref[...]
        lane = lax.broadcasted_iota(jnp.int32, cos.shape, 1) % HEAD_DIM
    for cb in range(y.shape[1] // 128):
        yb = y[:, cb * 128:(cb + 1) * 128]
        if rope:
            fwd = pltpu.roll(yb, 128 - ROT_DIM // 2, axis=1)
            bwd = pltpu.roll(yb, ROT_DIM // 2, axis=1)
            partner = jnp.where(lane < ROT_DIM // 2, fwd, bwd)
            yb = yb * cos + partner * sin
        if head_major:
            o_ref[0, cb, 0] = (yb * scale).T.astype(BF16)
        else:
            o_ref[:, cb * 128:(cb + 1) * 128] = (yb * scale).astype(BF16)


def _norm_proj(h2, g, w, cos_t, sin_t, *, rope, scale, lp, head_major=False):
    T, D = h2.shape
    N = w.shape[1]
    ts = PROJ_TS
    nt = lp // ts
    if head_major:
        out_spec = pl.BlockSpec((1, N // 128, 1, 128, ts), lambda i: (i // nt, 0, i % nt, 0, 0))
        out_shape = jax.ShapeDtypeStruct((T // lp, N // 128, nt, 128, ts), BF16)
    else:
        out_spec = pl.BlockSpec((ts, N), lambda i: (i, 0))
        out_shape = jax.ShapeDtypeStruct((T, N), BF16)
    return pl.pallas_call(
        functools.partial(_proj_kernel, rope=rope, scale=scale, head_major=head_major),
        grid=(T // ts,),
        in_specs=[
            pl.BlockSpec((ts, D), lambda i: (i, 0)),
            pl.BlockSpec((1, D), lambda i: (0, 0)),
            pl.BlockSpec((D, N), lambda i: (0, 0)),
            pl.BlockSpec((ts, 128), lambda i: (i % nt, 0)),
            pl.BlockSpec((ts, 128), lambda i: (i % nt, 0)),
        ],
        out_specs=out_spec,
        out_shape=out_shape,
        compiler_params=_cparams(("arbitrary",)),
        name="norm_proj",
    )(h2, g.reshape(1, D).astype(F32), w.astype(BF16), cos_t, sin_t)


def _rope_tables(lp):
    inv = ROPE_THETA ** (-jnp.arange(0, ROT_DIM, 2, dtype=F32) / ROT_DIM)
    d = jnp.arange(128) % HEAD_DIM
    ang = jnp.arange(lp, dtype=F32)[:, None] * inv[d % (ROT_DIM // 2)][None, :]
    cos_t = jnp.where(d[None, :] < ROT_DIM, jnp.cos(ang), 1.0)
    sin_t = jnp.where(d[None, :] < ROT_DIM // 2, -jnp.sin(ang),
                      jnp.where(d[None, :] < ROT_DIM, jnp.sin(ang), 0.0))
    return cos_t.astype(F32), sin_t.astype(F32)


ATT_T = 384
ATT_TK = 384
ATT_G = 4


def _attn_t_kernel(qt_ref, k_ref, vt_ref, lam_ref, sgt_ref, o_ref, *, lam_init):
    t = ATT_T
    qi = pl.program_id(2)
    qqs = []
    for g in range(ATT_G):
        qt = qt_ref[0, g, 0]
        row = lax.broadcasted_iota(jnp.int32, qt.shape, 0)
        zero = jnp.zeros_like(qt)
        qqs.append(jnp.concatenate([jnp.where(row < HEAD_DIM, qt, zero),
                                    jnp.where(row >= HEAD_DIM, qt, zero)], axis=1))

    tk = ATT_TK
    sub = t // tk

    def step(kc, carry, masked):
        k0 = pl.multiple_of(kc * tk, tk)
        kk = k_ref[0, pl.ds(k0, tk), :]
        ss = [_dot(kk[:, g * 128:(g + 1) * 128], qqs[g]) for g in range(ATT_G)]
        if masked:
            r = lax.broadcasted_iota(jnp.int32, (tk, 128), 0) + (k0 - qi * t)
            c = lax.broadcasted_iota(jnp.int32, (tk, 128), 1)
        out = []
        for g in range(ATT_G):
            m, l, acc = carry[g]
            ms, ls, ps = [], [], []
            for lb in range(2 * t // 128):
                sl = ss[g][:, lb * 128:(lb + 1) * 128]
                if masked:
                    sl = jnp.where(r <= c + (lb * 128) % t, sl, NEG)
                ml = m[:, lb * 128:(lb + 1) * 128]
                m_new = jnp.maximum(ml, jnp.max(sl, axis=0, keepdims=True))
                ms.append(m_new)
                ls.append(jnp.exp(ml - m_new))
                ps.append(jnp.exp(sl - m_new))
            m_new = jnp.concatenate(ms, axis=1)
            alpha = jnp.concatenate(ls, axis=1)
            p = jnp.concatenate(ps, axis=1)
            l = alpha * l + jnp.sum(p, axis=0, keepdims=True)
            acc = alpha * acc + _dot(vt_ref[0, g, kc], p.astype(BF16))
            out.append((m_new, l, acc))
        return tuple(out)

    init = tuple((jnp.full((1, 2 * t), NEG, F32), jnp.zeros((1, 2 * t), F32),
                  jnp.zeros((V_DIM, 2 * t), F32)) for _ in range(ATT_G))
    carry = lax.fori_loop(0, qi * sub, lambda kc, c: step(kc, c, False), init)
    for d in range(sub):
        carry = step(qi * sub + d, carry, True)
    lp = lam_ref[...]
    lam = (jnp.exp(jnp.sum(lp[0:1] * lp[1:2], axis=-1, keepdims=True))
           - jnp.exp(jnp.sum(lp[2:3] * lp[3:4], axis=-1, keepdims=True)) + lam_init)
    sgt = jnp.concatenate([sgt_ref[...]] * (t // 128), axis=1)
    for g in range(ATT_G):
        m, l, acc = carry[g]
        o = acc / l
        od = o[:, :t] - lam * o[:, t:]
        od = od * lax.rsqrt(jnp.mean(od * od, axis=0, keepdims=True) + EPS)
        od = od * sgt * (1.0 - lam_init)
        o_ref[0, :, g * 128:(g + 1) * 128] = od.T.astype(BF16)


def _diff_attention_t(qt, k3, vt, lam_rows, subln_g, lam_init):
    B, Lp, _ = k3.shape
    t, tk, g = ATT_T, ATT_TK, ATT_G
    nt, ntk = Lp // t, Lp // tk
    assert qt.shape == (B, N_HEADS, nt, 128, t) and vt.shape == (B, N_HEADS, ntk, V_DIM, tk)
    sgt = jnp.broadcast_to(subln_g.astype(F32)[:, None], (V_DIM, 128))
    return pl.pallas_call(
        functools.partial(_attn_t_kernel, lam_init=lam_init),
        grid=(B, N_HEADS // g, nt),
        in_specs=[
            pl.BlockSpec((1, g, 1, 128, t), lambda b, h, i: (b, h, i, 0, 0)),
            pl.BlockSpec((1, Lp, 128 * g), lambda b, h, i: (b, 0, h)),
            pl.BlockSpec((1, g, ntk, V_DIM, tk), lambda b, h, i: (b, h, 0, 0, 0)),
            pl.BlockSpec((8, 128), lambda b, h, i: (0, 0)),
            pl.BlockSpec((V_DIM, 128), lambda b, h, i: (0, 0)),
        ],
        out_specs=pl.BlockSpec((1, t, 128 * g), lambda b, h, i: (b, i, h)),
        out_shape=jax.ShapeDtypeStruct((B, Lp, N_HEADS * V_DIM), BF16),
        compiler_params=_cparams(("arbitrary", "arbitrary", "arbitrary")),
        name="diff_attn",
    )(qt, k3, vt, lam_rows, sgt)


def _matmul_res_kernel(a_ref, w_ref, h_ref, o_ref):
    o_ref[...] = h_ref[...] + _dot(a_ref[...], w_ref[...])


def _matmul_res(a2, w, h2):
    T, K = a2.shape
    N = w.shape[1]
    ts = PROJ_TS
    return pl.pallas_call(
        _matmul_res_kernel,
        grid=(T // ts,),
        in_specs=[pl.BlockSpec((ts, K), lambda i: (i, 0)),
                  pl.BlockSpec((K, N), lambda i: (0, 0)),
                  pl.BlockSpec((ts, N), lambda i: (i, 0))],
        out_specs=pl.BlockSpec((ts, N), lambda i: (i, 0)),
        out_shape=jax.ShapeDtypeStruct((T, N), F32),
        compiler_params=_cparams(("arbitrary",)),
        name="out_proj",
    )(a2, w.astype(BF16), h2)


ROUTE_TS = 256
RANK_SCALE = 2.0 ** 100
RANK_BIAS = 64.0


def _extract_top(s, n):
    rows = lax.broadcasted_iota(jnp.int32, (n, s.shape[1]), 0)
    vals = jnp.full((n, s.shape[1]), NEG, F32)
    work = s
    for j in range(n):
        m = jnp.max(work, axis=0, keepdims=True)
        vals = jnp.where(rows == j, m, vals)
        work = jnp.where(work >= m, -(RANK_BIAS + j) * RANK_SCALE, work)
    rank = jnp.where(work < -0.5 * RANK_BIAS * RANK_SCALE, work * (-1.0 / RANK_SCALE) - RANK_BIAS, float(n))
    return vals, rank


def _top_values(s, n):
    rows = lax.broadcasted_iota(jnp.int32, (n, s.shape[1]), 0)
    vals = jnp.full((n, s.shape[1]), NEG, F32)
    work = s
    for j in range(n):
        m = jnp.max(work, axis=0, keepdims=True)
        vals = jnp.where(rows == j, m, vals)
        if j + 1 < n:
            work = jnp.where(work >= m, NEG, work)
    return vals


def _candidate_sums(v1, v2):
    k = PEER_TOPK
    row8 = lax.broadcasted_iota(jnp.int32, (8, v1.shape[1]), 0)
    blocks = [v1[0:1] + v2]
    for a in range(1, 8):
        nb = k // (a + 1)
        blocks.append(jnp.where(row8 < nb, v1[a:a + 1] + v2[0:8], NEG))
    blocks.append(v1[8:16] + v2[0:1])
    return jnp.concatenate(blocks, axis=0)


def _route_kernel(h_ref, g_ref, wq_ref, sk1_ref, sk2_ref, xn_ref, c_ref, rk_ref, n_ref, r_ref):
    k = PEER_TOPK
    xn = _rms(h_ref[...], g_ref[...]).astype(BF16)
    xn_ref[...] = xn
    q = _dot(xn, wq_ref[...]).astype(BF16)
    for hd in range(PEER_HEADS):
        q1 = q[:, hd * 256:hd * 256 + 128]
        q2 = q[:, hd * 256 + 128:(hd + 1) * 256]
        s1 = _dot_nt(sk1_ref[...], q1)
        s2 = _dot_nt(sk2_ref[...], q2)
        v1, rank1 = _extract_top(s1, k)
        v2, rank2 = _extract_top(s2, k)
        tops = _top_values(_candidate_sums(v1, v2), k)
        z = jnp.sum(jnp.exp(tops - tops[0:1]), axis=0, keepdims=True)
        tau = tops[k - 1:k]
        cnt = jnp.zeros_like(v1)
        for b in range(k):
            cnt = cnt + jnp.where(v1 + v2[b:b + 1] >= tau, 1.0, 0.0)
        rank1_b = rank1.astype(BF16)
        nn = jnp.zeros(rank1_b.shape, BF16)
        for a in range(k):
            row = jnp.broadcast_to(cnt[a:a + 1].astype(BF16), nn.shape)
            nn = jnp.where(rank1_b == a, row, nn)
        n_ref[hd] = nn.astype(F32)
        r_ref[hd] = jnp.exp(s1 - v1[0:1]) * (0.5 / z)
        rk_ref[hd] = rank2.astype(BF16)
        c_ref[hd] = jnp.exp(s2 - v2[0:1]).astype(BF16)


def _peer_route(h2, g, wq, sk1, sk2):
    T, D = h2.shape
    ts = ROUTE_TS
    nq = wq.shape[1]
    tab32 = jax.ShapeDtypeStruct((PEER_HEADS, PEER_KEYS, T), F32)
    tab16 = jax.ShapeDtypeStruct((PEER_HEADS, PEER_KEYS, T), BF16)
    tab_spec = pl.BlockSpec((PEER_HEADS, PEER_KEYS, ts), lambda i: (0, 0, i))
    return pl.pallas_call(
        _route_kernel,
        grid=(T // ts,),
        in_specs=[pl.BlockSpec((ts, D), lambda i: (i, 0)),
                  pl.BlockSpec((1, D), lambda i: (0, 0)),
                  pl.BlockSpec((D, nq), lambda i: (0, 0)),
                  pl.BlockSpec((PEER_KEYS, 128), lambda i: (0, 0)),
                  pl.BlockSpec((PEER_KEYS, 128), lambda i: (0, 0))],
        out_specs=[pl.BlockSpec((ts, D), lambda i: (i, 0)), tab_spec, tab_spec, tab_spec, tab_spec],
        out_shape=[jax.ShapeDtypeStruct((T, D), BF16), tab16, tab16, tab32, tab32],
        compiler_params=_cparams(("arbitrary",)),
        name="peer_route",
    )(h2, g.reshape(1, D).astype(F32), wq.astype(BF16), sk1.astype(BF16), sk2.astype(BF16))


PEER_TT = 512
PEER_TC = 256
PEER_ET = 2048
SQRT_HALF = math.sqrt(0.5)


def _fake_zero(block):
    bits = pltpu.bitcast(block, jnp.uint32)
    return pltpu.bitcast((bits >> 16) >> 16, BF16)


def _peer_dense_kernel(xn_ref, u_ref, vt_ref, c_ref, rk_ref, n_ref, r_ref, h_ref, o_ref, acc_ref):
    j = pl.program_id(1)
    tiles_per_step = PEER_ET // PEER_KEYS

    @pl.when(j == 0)
    def _():
        acc_ref[...] = jnp.zeros_like(acc_ref)

    z = _dot_nt(u_ref[...], xn_ref[...])
    pieces = [[] for _ in range(PEER_TT // PEER_TC)]
    w = jnp.zeros((16, PEER_TC), BF16)
    for ii in range(tiles_per_step):
        for tc in range(PEER_TT // PEER_TC):
            tok = slice(tc * PEER_TC, (tc + 1) * PEER_TC)
            nb, rb = [], []
            base8 = pl.multiple_of(j * tiles_per_step + (ii // 8) * 8, 8)
            for hd in range(PEER_HEADS):
                n8 = n_ref[hd, pl.ds(base8, 8), tok]
                r8 = r_ref[hd, pl.ds(base8, 8), tok]
                nb.append(jnp.broadcast_to(n8[ii % 8:ii % 8 + 1], (16, PEER_TC)).astype(BF16))
                rb.append(jnp.broadcast_to(r8[ii % 8:ii % 8 + 1], (16, PEER_TC)).astype(BF16))
            for kb in range(PEER_KEYS // 16):
                rows = slice(kb * 16, (kb + 1) * 16)
                erow = slice(ii * PEER_KEYS + kb * 16, ii * PEER_KEYS + (kb + 1) * 16)
                w = _fake_zero(w)
                for hd in range(PEER_HEADS):
                    w = w + jnp.where(rk_ref[hd, rows, tok] < nb[hd], rb[hd] * c_ref[hd, rows, tok],
                                      jnp.zeros_like(w))
                zz = z[erow, tok]
                act = zz * (1.0 + lax.erf(zz * SQRT_HALF))
                pieces[tc].append(w * act.astype(BF16))
    p = jnp.concatenate([jnp.concatenate(col, axis=0) for col in pieces], axis=1)
    acc_ref[...] += _dot(vt_ref[...], p)

    @pl.when(j == pl.num_programs(1) - 1)
    def _():
        o_ref[...] = h_ref[...] + acc_ref[...].T


def _peer_dense(xn, u_bf, vt_bf, layer, c, rk, n, r, h2):
    T, D = h2.shape
    tt, et = PEER_TT, PEER_ET
    tab_spec = pl.BlockSpec((PEER_HEADS, PEER_KEYS, tt), lambda i, j: (0, 0, i))
    return pl.pallas_call(
        _peer_dense_kernel,
        grid=(T // tt, PEER_EXPERTS // et),
        in_specs=[pl.BlockSpec((tt, D), lambda i, j: (i, 0)),
                  pl.BlockSpec((None, et, D), lambda i, j: (layer, j, 0)),
                  pl.BlockSpec((None, D, et), lambda i, j: (layer, 0, j)),
                  tab_spec, tab_spec, tab_spec, tab_spec,
                  pl.BlockSpec((tt, D), lambda i, j: (i, 0))],
        out_specs=pl.BlockSpec((tt, D), lambda i, j: (i, 0)),
        out_shape=jax.ShapeDtypeStruct((T, D), F32),
        scratch_shapes=[pltpu.VMEM((D, tt), F32)],
        compiler_params=_cparams(("arbitrary", "arbitrary")),
        name="peer_dense",
    )(xn, u_bf, vt_bf, c, rk, n, r, h2)


def _peer_layer(h2, g, wq, sk1, sk2, u_bf, vt_bf, layer):
    xn, c, rk, n, r = _peer_route(h2, g, wq, sk1, sk2)
    return _peer_dense(xn, u_bf, vt_bf, layer, c, rk, n, r, h2)


def _final_norm_kernel(x_ref, g_ref, o_ref):
    o_ref[...] = _rms(x_ref[...], g_ref[...])


def _final_norm(h2, g):
    T, D = h2.shape
    ts = PROJ_TS
    return pl.pallas_call(
        _final_norm_kernel,
        grid=(T // ts,),
        in_specs=[pl.BlockSpec((ts, D), lambda i: (i, 0)), pl.BlockSpec((1, D), lambda i: (0, 0))],
        out_specs=pl.BlockSpec((ts, D), lambda i: (i, 0)),
        out_shape=jax.ShapeDtypeStruct((T, D), F32),
        compiler_params=_cparams(("arbitrary",)),
        name="final_norm",
    )(h2, g.reshape(1, D).astype(F32))


def kernel(x, meta_tokens, a_norm_g, a_pw1_w, a_pw1_b, a_dw_w, a_dw_b, a_ln_g, a_ln_b, a_pw2_w, a_pw2_b, kv_norm_g, w_kv, b_norm_g, b_wq, b_lambda_q1, b_lambda_k1, b_lambda_q2, b_lambda_k2, b_subln_g, b_wo, f_norm_g, f_wq, f_subkey1, f_subkey2, f_u, f_v, final_norm_g):
    B, S, D = x.shape
    L = N_META + S
    Lp = -(-L // BLOCK) * BLOCK
    T = B * Lp
    h = x
    u_bf = f_u.astype(BF16)
    vt_bf = jnp.swapaxes(f_v.astype(BF16), 1, 2)
    cos_t, sin_t = _rope_tables(Lp)
    nk = 2 * N_HEADS * HEAD_DIM
    k3 = vt = None
    for layer in range(DEPTH):
        if layer < N_A_LAYERS:
            i = layer
            h = _conformer_layer(h, a_norm_g[i], a_pw1_w[i], a_pw1_b[i], a_dw_w[i], a_dw_b[i],
                                 a_ln_g[i], a_ln_b[i], a_pw2_w[i], a_pw2_b[i],
                                 meta=meta_tokens if layer == 0 else None, lp=Lp)
        else:
            j = layer - N_A_LAYERS
            lam_init = 0.8 - 0.6 * math.exp(-0.3 * layer)
            h2 = h.reshape(T, D)
            qt = _norm_proj(h2, b_norm_g[j], b_wq[j], cos_t, sin_t, rope=True,
                            scale=HEAD_DIM ** -0.5, lp=Lp, head_major=True)
            lam_rows = jnp.zeros((8, 128), F32).at[0:4, 0:HEAD_DIM].set(
                jnp.stack([b_lambda_q1[j], b_lambda_k1[j], b_lambda_q2[j], b_lambda_k2[j]]).astype(F32))
            o = _diff_attention_t(qt, k3, vt, lam_rows, b_subln_g[j], lam_init)
            h = _matmul_res(o.reshape(T, N_HEADS * V_DIM), b_wo[j], h2).reshape(B, Lp, D)
        h2 = _peer_layer(h.reshape(T, D), f_norm_g[layer], f_wq[layer], f_subkey1[layer],
                         f_subkey2[layer], u_bf, vt_bf, layer)
        h = h2.reshape(B, Lp, D)
        if layer == N_A_LAYERS - 1:
            k3 = _norm_proj(h2, kv_norm_g, w_kv[:, :nk], cos_t, sin_t, rope=True, scale=1.0,
                            lp=Lp).reshape(B, Lp, nk)
            vt = _norm_proj(h2, kv_norm_g, w_kv[:, nk:], cos_t, sin_t, rope=False, scale=1.0,
                            lp=Lp, head_major=True)
    out = _final_norm(h.reshape(T, D), final_norm_g).reshape(B, Lp, D)
    return out[:, N_META:N_META + S]
```

```python
import functools
import math

import jax
import jax.numpy as jnp
from jax import lax
from jax.experimental import pallas as pl
from jax.experimental.pallas import tpu as pltpu

D_MODEL = 1024
N_META = 16
BLOCK = 128
DEPTH = 4
N_A_LAYERS = DEPTH // 2
CONV_WIDTH = 31
N_HEADS = 8
HEAD_DIM = 64
V_DIM = 128
ROT_DIM = 16
ROPE_THETA = 500000.0
PEER_HEADS = 8
PEER_KEYS = 128
PEER_EXPERTS = PEER_KEYS * PEER_KEYS
PEER_TOPK = 16
EPS = 1e-6

NEG = -1e30
VMEM_LIMIT = 52 * 1024 * 1024

F32 = jnp.float32
BF16 = jnp.bfloat16


def _cparams(sem):
    return pltpu.CompilerParams(dimension_semantics=sem, vmem_limit_bytes=VMEM_LIMIT)


def _rms(x, g):
    return x * lax.rsqrt(jnp.mean(x * x, axis=-1, keepdims=True) + EPS) * g


def _dot(a, b):
    return jnp.dot(a, b, preferred_element_type=F32)


def _dot_nt(a, b):
    return lax.dot_general(a, b, (((1,), (1,)), ((), ())), preferred_element_type=F32)


CONF_TS = 384
CONF_HALO = 32
CONF_RB = 32
CONF_CB = 256


def _conformer_kernel(x_ref, *rest):
    _conformer_body(x_ref[0], *rest)


def _conformer_first_kernel(xp_ref, xc_ref, meta_ref, *rest, seq):
    ts = CONF_TS
    t = pl.program_id(1)
    x = jnp.concatenate([xp_ref[0, ts - N_META:ts, :], xc_ref[0, 0:ts - N_META, :]], axis=0)
    grow = t * ts + lax.broadcasted_iota(jnp.int32, (ts, 1), 0)
    meta = jnp.concatenate([meta_ref[...], jnp.zeros((ts - N_META, D_MODEL), F32)], axis=0)
    x = jnp.where(grow < N_META, meta, jnp.where(grow < N_META + seq, x, 0.0))
    _conformer_body(x, *rest)


def _conformer_body(x, ng_ref, w1a_ref, w1g_ref, b1a_ref, b1g_ref, dww_ref, dwb_ref,
                    lng_ref, lnb_ref, w2_ref, b2_ref, o_ref, buf_ref, cbuf_ref):
    ts = CONF_TS

    @pl.when(pl.program_id(1) == 0)
    def _():
        buf_ref[0:CONF_HALO, :] = jnp.zeros((CONF_HALO, D_MODEL), F32)

    n = _rms(x, ng_ref[...]).astype(BF16)
    a = _dot(n, w1a_ref[...]) + b1a_ref[...]
    g = _dot(n, w1g_ref[...]) + b1g_ref[...]
    buf_ref[CONF_HALO:CONF_HALO + ts, :] = a * jax.nn.sigmoid(g)

    off0 = CONF_HALO - (CONV_WIDTH - 1)

    def row_body(rb, carry):
        r0 = pl.multiple_of(rb * CONF_RB, CONF_RB)
        for cb in range(D_MODEL // CONF_CB):
            cols = slice(cb * CONF_CB, (cb + 1) * CONF_CB)
            acc = jnp.broadcast_to(dwb_ref[:, cols], (CONF_RB, CONF_CB))
            win = buf_ref[pl.ds(r0, CONF_RB + CONF_HALO), cols]
            shifted = [win] + [pltpu.roll(win, CONF_RB + CONF_HALO - j, axis=0) for j in range(1, 8)]
            for k in range(CONV_WIDTH):
                q8, j = divmod(off0 + k, 8)
                acc = acc + dww_ref[k:k + 1, cols] * shifted[j][8 * q8:8 * q8 + CONF_RB]
            cbuf_ref[pl.ds(r0, CONF_RB), cols] = acc
        return carry

    lax.fori_loop(0, ts // CONF_RB, row_body, 0)
    buf_ref[0:CONF_HALO, :] = buf_ref[ts:ts + CONF_HALO, :]

    c = cbuf_ref[...]
    mu = jnp.mean(c, axis=-1, keepdims=True)
    cc = c - mu
    y = cc * lax.rsqrt(jnp.mean(cc * cc, axis=-1, keepdims=True) + EPS)
    y = y * lng_ref[...] + lnb_ref[...]
    y = (y * jax.nn.sigmoid(y)).astype(BF16)
    o_ref[0] = x + _dot(y, w2_ref[...]) + b2_ref[...]


def _conformer_layer(h3, ng, w1, b1, dww, dwb, lng, lnb, w2, b2, *, meta=None, lp=None):
    B, _, D = h3.shape
    Lp = h3.shape[1] if meta is None else lp
    ts = CONF_TS
    row = lambda v: v.reshape(1, -1).astype(F32)
    const = lambda shape: pl.BlockSpec(shape, lambda b, t: (0,) * len(shape))
    if meta is None:
        body, lead_specs, lead = _conformer_kernel, [pl.BlockSpec((1, ts, D), lambda b, t: (b, t, 0))], (h3,)
    else:
        last = (h3.shape[1] - 1) // ts
        body = functools.partial(_conformer_first_kernel, seq=h3.shape[1])
        lead_specs = [pl.BlockSpec((1, ts, D), lambda b, t: (b, jnp.maximum(t - 1, 0), 0)),
                      pl.BlockSpec((1, ts, D), lambda b, t: (b, jnp.minimum(t, last), 0)),
                      const((N_META, D))]
        lead = (h3, h3, meta.astype(F32))
    return pl.pallas_call(
        body,
        grid=(B, Lp // ts),
        in_specs=lead_specs + [
            const((1, D)), const((D, D)), const((D, D)), const((1, D)), const((1, D)),
            const((CONV_WIDTH, D)), const((1, D)), const((1, D)), const((1, D)),
            const((D, D)), const((1, D)),
        ],
        out_specs=pl.BlockSpec((1, ts, D), lambda b, t: (b, t, 0)),
        out_shape=jax.ShapeDtypeStruct((B, Lp, D), F32),
        scratch_shapes=[pltpu.VMEM((ts + CONF_HALO, D), F32), pltpu.VMEM((ts, D), F32)],
        compiler_params=_cparams(("arbitrary", "arbitrary")),
        name="conformer",
    )(*lead, row(ng), w1[:, :D].astype(BF16), w1[:, D:].astype(BF16), row(b1[:D]), row(b1[D:]),
      dww.astype(F32), row(dwb), row(lng), row(lnb), w2.astype(BF16), row(b2))


PROJ_TS = 384


def _proj_kernel(x_ref, g_ref, w_ref, cos_ref, sin_ref, o_ref, *, rope, scale, head_major):
    n = _rms(x_ref[...], g_ref[...]).astype(BF16)
    y = _dot(n, w_ref[...])
    if rope:
        cos = cos_ref[...]
        sin = sin_ref[...]
        lane = lax.broadcasted_iota(jnp.int32, cos.shape, 1) % HEAD_DIM
    for cb in range(y.shape[1] // 128):
        yb = y[:, cb * 128:(cb + 1) * 128]
        if rope:
            fwd = pltpu.roll(yb, 128 - ROT_DIM // 2, axis=1)
            bwd = pltpu.roll(yb, ROT_DIM // 2, axis=1)
            partner = jnp.where(lane < ROT_DIM // 2, fwd, bwd)
            yb = yb * cos + partner * sin
        if head_major:
            o_ref[0, cb, 0] = (yb * scale).T.astype(BF16)
        else:
            o_ref[:, cb * 128:(cb + 1) * 128] = (yb * scale).astype(BF16)


def _norm_proj(h2, g, w, cos_t, sin_t, *, rope, scale, lp, head_major=False):
    T, D = h2.shape
    N = w.shape[1]
    ts = PROJ_TS
    nt = lp // ts
    if head_major:
        out_spec = pl.BlockSpec((1, N // 128, 1, 128, ts), lambda i: (i // nt, 0, i % nt, 0, 0))
        out_shape = jax.ShapeDtypeStruct((T // lp, N // 128, nt, 128, ts), BF16)
    else:
        out_spec = pl.BlockSpec((ts, N), lambda i: (i, 0))
        out_shape = jax.ShapeDtypeStruct((T, N), BF16)
    return pl.pallas_call(
        functools.partial(_proj_kernel, rope=rope, scale=scale, head_major=head_major),
        grid=(T // ts,),
        in_specs=[
            pl.BlockSpec((ts, D), lambda i: (i, 0)),
            pl.BlockSpec((1, D), lambda i: (0, 0)),
            pl.BlockSpec((D, N), lambda i: (0, 0)),
            pl.BlockSpec((ts, 128), lambda i: (i % nt, 0)),
            pl.BlockSpec((ts, 128), lambda i: (i % nt, 0)),
        ],
        out_specs=out_spec,
        out_shape=out_shape,
        compiler_params=_cparams(("arbitrary",)),
        name="norm_proj",
    )(h2, g.reshape(1, D).astype(F32), w.astype(BF16), cos_t, sin_t)


def _rope_tables(lp):
    inv = ROPE_THETA ** (-jnp.arange(0, ROT_DIM, 2, dtype=F32) / ROT_DIM)
    d = jnp.arange(128) % HEAD_DIM
    ang = jnp.arange(lp, dtype=F32)[:, None] * inv[d % (ROT_DIM // 2)][None, :]
    cos_t = jnp.where(d[None, :] < ROT_DIM, jnp.cos(ang), 1.0)
    sin_t = jnp.where(d[None, :] < ROT_DIM // 2, -jnp.sin(ang),
                      jnp.where(d[None, :] < ROT_DIM, jnp.sin(ang), 0.0))
    return cos_t.astype(F32), sin_t.astype(F32)


ATT_T = 384
ATT_TK = 384
ATT_G = 4


def _attn_t_kernel(qt_ref, k_ref, vt_ref, lam_ref, sgt_ref, o_ref, *, lam_init):
    t = ATT_T
    qi = pl.program_id(2)
    qqs = []
    for g in range(ATT_G):
        qt = qt_ref[0, g, 0]
        row = lax.broadcasted_iota(jnp.int32, qt.shape, 0)
        zero = jnp.zeros_like(qt)
        qqs.append(jnp.concatenate([jnp.where(row < HEAD_DIM, qt, zero),
                                    jnp.where(row >= HEAD_DIM, qt, zero)], axis=1))

    tk = ATT_TK
    sub = t // tk

    def step(kc, carry, masked):
        k0 = pl.multiple_of(kc * tk, tk)
        kk = k_ref[0, pl.ds(k0, tk), :]
        ss = [_dot(kk[:, g * 128:(g + 1) * 128], qqs[g]) for g in range(ATT_G)]
        if masked:
            r = lax.broadcasted_iota(jnp.int32, (tk, 128), 0) + (k0 - qi * t)
            c = lax.broadcasted_iota(jnp.int32, (tk, 128), 1)
        out = []
        for g in range(ATT_G):
            m, l, acc = carry[g]
            ms, ls, ps = [], [], []
            for lb in range(2 * t // 128):
                sl = ss[g][:, lb * 128:(lb + 1) * 128]
                if masked:
                    sl = jnp.where(r <= c + (lb * 128) % t, sl, NEG)
                ml = m[:, lb * 128:(lb + 1) * 128]
                m_new = jnp.maximum(ml, jnp.max(sl, axis=0, keepdims=True))
                ms.append(m_new)
                ls.append(jnp.exp(ml - m_new))
                ps.append(jnp.exp(sl - m_new))
            m_new = jnp.concatenate(ms, axis=1)
            alpha = jnp.concatenate(ls, axis=1)
            p = jnp.concatenate(ps, axis=1)
            l = alpha * l + jnp.sum(p, axis=0, keepdims=True)
            acc = alpha * acc + _dot(vt_ref[0, g, kc], p.astype(BF16))
            out.append((m_new, l, acc))
        return tuple(out)

    init = tuple((jnp.full((1, 2 * t), NEG, F32), jnp.zeros((1, 2 * t), F32),
                  jnp.zeros((V_DIM, 2 * t), F32)) for _ in range(ATT_G))
    carry = lax.fori_loop(0, qi * sub, lambda kc, c: step(kc, c, False), init)
    for d in range(sub):
        carry = step(qi * sub + d, carry, True)
    lp = lam_ref[...]
    lam = (jnp.exp(jnp.sum(lp[0:1] * lp[1:2], axis=-1, keepdims=True))
           - jnp.exp(jnp.sum(lp[2:3] * lp[3:4], axis=-1, keepdims=True)) + lam_init)
    sgt = jnp.concatenate([sgt_ref[...]] * (t // 128), axis=1)
    for g in range(ATT_G):
        m, l, acc = carry[g]
        o = acc / l
        od = o[:, :t] - lam * o[:, t:]
        od = od * lax.rsqrt(jnp.mean(od * od, axis=0, keepdims=True) + EPS)
        od = od * sgt * (1.0 - lam_init)
        o_ref[0, :, g * 128:(g + 1) * 128] = od.T.astype(BF16)


def _diff_attention_t(qt, k3, vt, lam_rows, subln_g, lam_init):
    B, Lp, _ = k3.shape
    t, tk, g = ATT_T, ATT_TK, ATT_G
    nt, ntk = Lp // t, Lp // tk
    assert qt.shape == (B, N_HEADS, nt, 128, t) and vt.shape == (B, N_HEADS, ntk, V_DIM, tk)
    sgt = jnp.broadcast_to(subln_g.astype(F32)[:, None], (V_DIM, 128))
    return pl.pallas_call(
        functools.partial(_attn_t_kernel, lam_init=lam_init),
        grid=(B, N_HEADS // g, nt),
        in_specs=[
            pl.BlockSpec((1, g, 1, 128, t), lambda b, h, i: (b, h, i, 0, 0)),
            pl.BlockSpec((1, Lp, 128 * g), lambda b, h, i: (b, 0, h)),
            pl.BlockSpec((1, g, ntk, V_DIM, tk), lambda b, h, i: (b, h, 0, 0, 0)),
            pl.BlockSpec((8, 128), lambda b, h, i: (0, 0)),
            pl.BlockSpec((V_DIM, 128), lambda b, h, i: (0, 0)),
        ],
        out_specs=pl.BlockSpec((1, t, 128 * g), lambda b, h, i: (b, i, h)),
        out_shape=jax.ShapeDtypeStruct((B, Lp, N_HEADS * V_DIM), BF16),
        compiler_params=_cparams(("arbitrary", "arbitrary", "arbitrary")),
        name="diff_attn",
    )(qt, k3, vt, lam_rows, sgt)


def _matmul_res_kernel(a_ref, w_ref, h_ref, o_ref):
    o_ref[...] = h_ref[...] + _dot(a_ref[...], w_ref[...])


def _matmul_res(a2, w, h2):
    T, K = a2.shape
    N = w.shape[1]
    ts = PROJ_TS
    return pl.pallas_call(
        _matmul_res_kernel,
        grid=(T // ts,),
        in_specs=[pl.BlockSpec((ts, K), lambda i: (i, 0)),
                  pl.BlockSpec((K, N), lambda i: (0, 0)),
                  pl.BlockSpec((ts, N), lambda i: (i, 0))],
        out_specs=pl.BlockSpec((ts, N), lambda i: (i, 0)),
        out_shape=jax.ShapeDtypeStruct((T, N), F32),
        compiler_params=_cparams(("arbitrary",)),
        name="out_proj",
    )(a2, w.astype(BF16), h2)


ROUTE_TS = 256
RANK_SCALE = 2.0 ** 100
RANK_BIAS = 64.0


def _extract_top(s, n):
    rows = lax.broadcasted_iota(jnp.int32, (n, s.shape[1]), 0)
    vals = jnp.full((n, s.shape[1]), NEG, F32)
    work = s
    for j in range(n):
        m = jnp.max(work, axis=0, keepdims=True)
        vals = jnp.where(rows == j, m, vals)
        work = jnp.where(work >= m, -(RANK_BIAS + j) * RANK_SCALE, work)
    rank = jnp.where(work < -0.5 * RANK_BIAS * RANK_SCALE, work * (-1.0 / RANK_SCALE) - RANK_BIAS, float(n))
    return vals, rank


def _top_values(s, n):
    rows = lax.broadcasted_iota(jnp.int32, (n, s.shape[1]), 0)
    vals = jnp.full((n, s.shape[1]), NEG, F32)
    work = s
    for j in range(n):
        m = jnp.max(work, axis=0, keepdims=True)
        vals = jnp.where(rows == j, m, vals)
        if j + 1 < n:
            work = jnp.where(work >= m, NEG, work)
    return vals


def _candidate_sums(v1, v2):
    k = PEER_TOPK
    row8 = lax.broadcasted_iota(jnp.int32, (8, v1.shape[1]), 0)
    blocks = [v1[0:1] + v2]
    for a in range(1, 8):
        nb = k // (a + 1)
        blocks.append(jnp.where(row8 < nb, v1[a:a + 1] + v2[0:8], NEG))
    blocks.append(v1[8:16] + v2[0:1])
    return jnp.concatenate(blocks, axis=0)


def _route_kernel(h_ref, g_ref, wq_ref, sk1_ref, sk2_ref, xn_ref, c_ref, rk_ref, n_ref, r_ref):
    k = PEER_TOPK
    xn = _rms(h_ref[...], g_ref[...]).astype(BF16)
    xn_ref[...] = xn
    q = _dot(xn, wq_ref[...]).astype(BF16)
    for hd in range(PEER_HEADS):
        q1 = q[:, hd * 256:hd * 256 + 128]
        q2 = q[:, hd * 256 + 128:(hd + 1) * 256]
        s1 = _dot_nt(sk1_ref[...], q1)
        s2 = _dot_nt(sk2_ref[...], q2)
        v1, rank1 = _extract_top(s1, k)
        v2, rank2 = _extract_top(s2, k)
        tops = _top_values(_candidate_sums(v1, v2), k)
        z = jnp.sum(jnp.exp(tops - tops[0:1]), axis=0, keepdims=True)
        tau = tops[k - 1:k]
        cnt = jnp.zeros_like(v1)
        for b in range(k):
            cnt = cnt + jnp.where(v1 + v2[b:b + 1] >= tau, 1.0, 0.0)
        rank1_b = rank1.astype(BF16)
        nn = jnp.zeros(rank1_b.shape, BF16)
        for a in range(k):
            row = jnp.broadcast_to(cnt[a:a + 1].astype(BF16), nn.shape)
            nn = jnp.where(rank1_b == a, row, nn)
        n_ref[hd] = nn.astype(F32)
        r_ref[hd] = jnp.exp(s1 - v1[0:1]) * (0.5 / z)
        rk_ref[hd] = rank2.astype(BF16)
        c_ref[hd] = jnp.exp(s2 - v2[0:1]).astype(BF16)


def _peer_route(h2, g, wq, sk1, sk2):
    T, D = h2.shape
    ts = ROUTE_TS
    nq = wq.shape[1]
    tab32 = jax.ShapeDtypeStruct((PEER_HEADS, PEER_KEYS, T), F32)
    tab16 = jax.ShapeDtypeStruct((PEER_HEADS, PEER_KEYS, T), BF16)
    tab_spec = pl.BlockSpec((PEER_HEADS, PEER_KEYS, ts), lambda i: (0, 0, i))
    return pl.pallas_call(
        _route_kernel,
        grid=(T // ts,),
        in_specs=[pl.BlockSpec((ts, D), lambda i: (i, 0)),
                  pl.BlockSpec((1, D), lambda i: (0, 0)),
                  pl.BlockSpec((D, nq), lambda i: (0, 0)),
                  pl.BlockSpec((PEER_KEYS, 128), lambda i: (0, 0)),
                  pl.BlockSpec((PEER_KEYS, 128), lambda i: (0, 0))],
        out_specs=[pl.BlockSpec((ts, D), lambda i: (i, 0)), tab_spec, tab_spec, tab_spec, tab_spec],
        out_shape=[jax.ShapeDtypeStruct((T, D), BF16), tab16, tab16, tab32, tab32],
        compiler_params=_cparams(("arbitrary",)),
        name="peer_route",
    )(h2, g.reshape(1, D).astype(F32), wq.astype(BF16), sk1.astype(BF16), sk2.astype(BF16))


PEER_TT = 512
PEER_TC = 256
PEER_ET = 2048
SQRT_HALF = math.sqrt(0.5)


def _fake_zero(block):
    bits = pltpu.bitcast(block, jnp.uint32)
    return pltpu.bitcast((bits >> 16) >> 16, BF16)


def _peer_dense_kernel(xn_ref, u_ref, vt_ref, c_ref, rk_ref, n_ref, r_ref, h_ref, o_ref, acc_ref):
    j = pl.program_id(1)
    tiles_per_step = PEER_ET // PEER_KEYS

    @pl.when(j == 0)
    def _():
        acc_ref[...] = jnp.zeros_like(acc_ref)

    z = _dot_nt(u_ref[...], xn_ref[...])
    pieces = [[] for _ in range(PEER_TT // PEER_TC)]
    w = jnp.zeros((16, PEER_TC), BF16)
    for ii in range(tiles_per_step):
        for tc in range(PEER_TT // PEER_TC):
            tok = slice(tc * PEER_TC, (tc + 1) * PEER_TC)
            nb, rb = [], []
            base8 = pl.multiple_of(j * tiles_per_step + (ii // 8) * 8, 8)
            for hd in range(PEER_HEADS):
                n8 = n_ref[hd, pl.ds(base8, 8), tok]
                r8 = r_ref[hd, pl.ds(base8, 8), tok]
                nb.append(jnp.broadcast_to(n8[ii % 8:ii % 8 + 1], (16, PEER_TC)).astype(BF16))
                rb.append(jnp.broadcast_to(r8[ii % 8:ii % 8 + 1], (16, PEER_TC)).astype(BF16))
            for kb in range(PEER_KEYS // 16):
                rows = slice(kb * 16, (kb + 1) * 16)
                erow = slice(ii * PEER_KEYS + kb * 16, ii * PEER_KEYS + (kb + 1) * 16)
                w = _fake_zero(w)
                for hd in range(PEER_HEADS):
                    w = w + jnp.where(rk_ref[hd, rows, tok] < nb[hd], rb[hd] * c_ref[hd, rows, tok],
                                      jnp.zeros_like(w))
                zz = z[erow, tok]
                act = zz * (1.0 + lax.erf(zz * SQRT_HALF))
                pieces[tc].append(w * act.astype(BF16))
    p = jnp.concatenate([jnp.concatenate(col, axis=0) for col in pieces], axis=1)
    acc_ref[...] += _dot(vt_ref[...], p)

    @pl.when(j == pl.num_programs(1) - 1)
    def _():
        o_ref[...] = h_ref[...] + acc_ref[...].T


def _peer_dense(xn, u_bf, vt_bf, layer, c, rk, n, r, h2):
    T, D = h2.shape
    tt, et = PEER_TT, PEER_ET
    tab_spec = pl.BlockSpec((PEER_HEADS, PEER_KEYS, tt), lambda i, j: (0, 0, i))
    return pl.pallas_call(
        _peer_dense_kernel,
        grid=(T // tt, PEER_EXPERTS // et),
        in_specs=[pl.BlockSpec((tt, D), lambda i, j: (i, 0)),
                  pl.BlockSpec((None, et, D), lambda i, j: (layer, j, 0)),
                  pl.BlockSpec((None, D, et), lambda i, j: (layer, 0, j)),
                  tab_spec, tab_spec, tab_spec, tab_spec,
                  pl.BlockSpec((tt, D), lambda i, j: (i, 0))],
        out_specs=pl.BlockSpec((tt, D), lambda i, j: (i, 0)),
        out_shape=jax.ShapeDtypeStruct((T, D), F32),
        scratch_shapes=[pltpu.VMEM((D, tt), F32)],
        compiler_params=_cparams(("arbitrary", "arbitrary")),
        name="peer_dense",
    )(xn, u_bf, vt_bf, c, rk, n, r, h2)


def _peer_layer(h2, g, wq, sk1, sk2, u_bf, vt_bf, layer):
    xn, c, rk, n, r = _peer_route(h2, g, wq, sk1, sk2)
    return _peer_dense(xn, u_bf, vt_bf, layer, c, rk, n, r, h2)


FINAL_TS = 512


def _final_norm_shift_kernel(cur_ref, nxt_ref, g_ref, o_ref):
    x = jnp.concatenate([cur_ref[0, N_META:FINAL_TS, :], nxt_ref[0]], axis=0)
    o_ref[0] = _rms(x, g_ref[...])


def _final_norm(h3, g, seq):
    B, Lp, D = h3.shape
    ts = FINAL_TS
    assert seq % ts == 0 and ts % N_META == 0 and seq + N_META <= Lp
    return pl.pallas_call(
        _final_norm_shift_kernel,
        grid=(B, seq // ts),
        in_specs=[pl.BlockSpec((1, ts, D), lambda b, t: (b, t, 0)),
                  pl.BlockSpec((1, N_META, D), lambda b, t: (b, (t + 1) * (ts // N_META), 0)),
                  pl.BlockSpec((1, D), lambda b, t: (0, 0))],
        out_specs=pl.BlockSpec((1, ts, D), lambda b, t: (b, t, 0)),
        out_shape=jax.ShapeDtypeStruct((B, seq, D), F32),
        compiler_params=_cparams(("arbitrary", "arbitrary")),
        name="final_norm",
    )(h3, h3, g.reshape(1, D).astype(F32))


def kernel(x, meta_tokens, a_norm_g, a_pw1_w, a_pw1_b, a_dw_w, a_dw_b, a_ln_g, a_ln_b, a_pw2_w, a_pw2_b, kv_norm_g, w_kv, b_norm_g, b_wq, b_lambda_q1, b_lambda_k1, b_lambda_q2, b_lambda_k2, b_subln_g, b_wo, f_norm_g, f_wq, f_subkey1, f_subkey2, f_u, f_v, final_norm_g):
    B, S, D = x.shape
    L = N_META + S
    Lp = -(-L // BLOCK) * BLOCK
    T = B * Lp
    h = x
    u_bf = f_u.astype(BF16)
    vt_bf = jnp.swapaxes(f_v.astype(BF16), 1, 2)
    cos_t, sin_t = _rope_tables(Lp)
    nk = 2 * N_HEADS * HEAD_DIM
    k3 = vt = None
    for layer in range(DEPTH):
        if layer < N_A_LAYERS:
            i = layer
            h = _conformer_layer(h, a_norm_g[i], a_pw1_w[i], a_pw1_b[i], a_dw_w[i], a_dw_b[i],
                                 a_ln_g[i], a_ln_b[i], a_pw2_w[i], a_pw2_b[i],
                                 meta=meta_tokens if layer == 0 else None, lp=Lp)
        else:
            j = layer - N_A_LAYERS
            lam_init = 0.8 - 0.6 * math.exp(-0.3 * layer)
            h2 = h.reshape(T, D)
            qt = _norm_proj(h2, b_norm_g[j], b_wq[j], cos_t, sin_t, rope=True,
                            scale=HEAD_DIM ** -0.5, lp=Lp, head_major=True)
            lam_rows = jnp.zeros((8, 128), F32).at[0:4, 0:HEAD_DIM].set(
                jnp.stack([b_lambda_q1[j], b_lambda_k1[j], b_lambda_q2[j], b_lambda_k2[j]]).astype(F32))
            o = _diff_attention_t(qt, k3, vt, lam_rows, b_subln_g[j], lam_init)
            h = _matmul_res(o.reshape(T, N_HEADS * V_DIM), b_wo[j], h2).reshape(B, Lp, D)
        h2 = _peer_layer(h.reshape(T, D), f_norm_g[layer], f_wq[layer], f_subkey1[layer],
                         f_subkey2[layer], u_bf, vt_bf, layer)
        h = h2.reshape(B, Lp, D)
        if layer == N_A_LAYERS - 1:
            k3 = _norm_proj(h2, kv_norm_g, w_kv[:, :nk], cos_t, sin_t, rope=True, scale=1.0,
                            lp=Lp).reshape(B, Lp, nk)
            vt = _norm_proj(h2, kv_norm_g, w_kv[:, nk:], cos_t, sin_t, rope=False, scale=1.0,
                            lp=Lp, head_major=True)
    return _final_norm(h, final_norm_g, S)
```
